```python
import math
import jax, jax.numpy as jnp
from jax import lax
import numpy as np

D_MODEL = 1024
BATCH = 8
SEQ = 2048
DEPTH = 1

HEAD_DIM = 64
D_ATTN = D_MODEL // 2
D_RWKV = D_MODEL - D_ATTN
N_Q_HEADS = D_ATTN // HEAD_DIM
N_KV_HEADS = max(1, N_Q_HEADS // 4)
Q_PER_KV = N_Q_HEADS // N_KV_HEADS
D_KV = N_KV_HEADS * HEAD_DIM
WINDOW = 128
BLOCK = 128
N_BUCKETS = 32
MAX_DISTANCE = 128
N_RWKV_HEADS = D_RWKV // HEAD_DIM
LORA_DECAY = 64
LORA_ICLR = 64
LORA_GATE = 128
RWKV_COLS = 3 * D_RWKV + LORA_DECAY + LORA_ICLR + LORA_GATE
RWKV_SPLITS = (D_RWKV, 2 * D_RWKV, 3 * D_RWKV, 3 * D_RWKV + LORA_DECAY, 3 * D_RWKV + LORA_DECAY + LORA_ICLR)
D_IN = D_ATTN + 2 * D_KV + RWKV_COLS
D_FF = 4 * D_MODEL
CONV_WIDTH = 3
NORM_EPS = 1e-6
GN_EPS = 64e-5
NEG_INF = -1e30

kernel_name = "hymba_swa_sink_rwkv7_convffn_sandwich"


def rms_norm(x, g):
    xf = x.astype(jnp.float32)
    y = xf * lax.rsqrt(jnp.mean(xf * xf, axis=-1, keepdims=True) + NORM_EPS) * g.astype(jnp.float32)
    return y.astype(x.dtype)


def t5_bucket(rel):
    n = jnp.maximum(rel, 0)
    max_exact = N_BUCKETS // 2
    large = max_exact + (jnp.log(jnp.maximum(n, 1).astype(jnp.float32) / max_exact)
                         / math.log(MAX_DISTANCE / max_exact) * (N_BUCKETS - max_exact)).astype(jnp.int32)
    large = jnp.minimum(large, N_BUCKETS - 1)
    return jnp.where(n < max_exact, n, large)


def sliding_window_sink_attention(q, k, v, rel_bias, sinks):
    B, S, _ = q.shape
    NB = S // BLOCK
    q = q.reshape(B, NB, BLOCK, N_KV_HEADS, Q_PER_KV, HEAD_DIM)

    def band(t):
        t = t.reshape(B, NB, BLOCK, N_KV_HEADS, HEAD_DIM)
        prev = jnp.concatenate([jnp.zeros_like(t[:, :1]), t[:, :-1]], axis=1)
        return jnp.concatenate([prev, t], axis=2)

    kb, vb = band(k), band(v)
    rel = (jnp.arange(BLOCK)[:, None] + BLOCK) - jnp.arange(2 * BLOCK)[None, :]
    in_window = (rel >= 0) & (rel < WINDOW)
    key_pos = (jnp.arange(NB)[:, None] - 1) * BLOCK + jnp.arange(2 * BLOCK)[None, :]
    mask = in_window[None] & (key_pos >= 0)[:, None, :]
    bias = rel_bias.astype(jnp.float32)[t5_bucket(rel)]
    bias = bias.transpose(2, 0, 1).reshape(N_KV_HEADS, Q_PER_KV, BLOCK, 2 * BLOCK)
    scores = jnp.einsum('bnqhgd,bnkhd->bnhgqk', q, kb).astype(jnp.float32) * (HEAD_DIM ** -0.5) + bias
    scores = jnp.where(mask[None, :, None, None], scores, NEG_INF)
    sink = sinks.astype(jnp.float32).reshape(N_KV_HEADS, Q_PER_KV)[:, :, None, None]
    m = jnp.maximum(scores.max(axis=-1, keepdims=True), sink)
    p = jnp.exp(scores - m)
    probs = p / (p.sum(axis=-1, keepdims=True) + jnp.exp(sink - m))
    out = jnp.einsum('bnhgqk,bnkhd->bnqhgd', probs.astype(v.dtype), vb)
    return out.reshape(B, S, D_ATTN)


def rwkv7_time_mix(p, w0, w_decay_up, a0, w_iclr_up, w_gate_up, k_k, k_a, r_k, ln_x_g, ln_x_b):
    B, S, _ = p.shape
    H, N = N_RWKV_HEADS, HEAD_DIM
    p = p.astype(jnp.float32)
    r, k, v, zw, za, zg = jnp.split(p, RWKV_SPLITS, axis=-1)
    w_log = -jax.nn.softplus(-(w0 + jnp.tanh(zw) @ w_decay_up)) - 0.5
    decay = jnp.exp(-jnp.exp(w_log))
    a = jax.nn.sigmoid(a0 + za @ w_iclr_up)
    g = jax.nn.sigmoid(zg) @ w_gate_up
    kk = (k * k_k).reshape(B, S, H, N)
    kk = kk / jnp.maximum(jnp.sqrt(jnp.sum(kk * kk, axis=-1, keepdims=True)), 1e-12)
    k = k * (1.0 + (a - 1.0) * k_a)

    def heads(t):
        return t.reshape(B, S, H, N)

    def tmaj(t):
        return t.swapaxes(0, 1)

    rh, kh, vh = heads(r), heads(k), heads(v)

    def step(state, inp):
        r_t, w_t, k_t, v_t, kk_t, a_t = inp
        sa = jnp.einsum('bhvk,bhk->bhv', state, -kk_t)
        state = (state * w_t[:, :, None, :]
                 + sa[..., None] * (kk_t * a_t)[:, :, None, :]
                 + v_t[..., None] * k_t[:, :, None, :])
        return state, jnp.einsum('bhvk,bhk->bhv', state, r_t)

    state0 = jnp.zeros((B, H, N, N), jnp.float32)
    _, o = lax.scan(step, state0, (tmaj(rh), tmaj(heads(decay)), tmaj(kh), tmaj(vh), tmaj(kk), tmaj(heads(a))))
    o = o.swapaxes(0, 1)
    mu = jnp.mean(o, axis=-1, keepdims=True)
    var = jnp.mean(jnp.square(o - mu), axis=-1, keepdims=True)
    o = ((o - mu) * lax.rsqrt(var + GN_EPS)).reshape(B, S, D_RWKV) * ln_x_g + ln_x_b
    bonus = jnp.sum(rh * kh * r_k, axis=-1, keepdims=True) * vh
    o = o + bonus.reshape(B, S, D_RWKV)
    return o * g


def conv_gated_ffn(h, w_up, conv_w, conv_b, w_down):
    S = h.shape[1]
    u = h @ w_up
    u_pad = jnp.pad(u, ((0, 0), (CONV_WIDTH - 1, 0), (0, 0)))
    u = conv_b + sum(conv_w[j] * u_pad[:, j:j + S] for j in range(CONV_WIDTH))
    gate, val = jnp.split(u, 2, axis=-1)
    return (jax.nn.gelu(gate, approximate=True) * val) @ w_down


def setup_inputs(seed: int = 0) -> dict:
    key = jax.random.key(seed)
    ks = jax.random.split(key, 24)
    L = DEPTH
    nrm = lambda k, shape, s: jax.random.normal(k, shape, jnp.float32) * s
    return {
        "x": jax.random.normal(ks[0], (BATCH, SEQ, D_MODEL), jnp.float32),
        "norm_mix_pre": 1.0 + nrm(ks[1], (L, D_MODEL), 0.02),
        "norm_mix_post": 1.0 + nrm(ks[2], (L, D_MODEL), 0.02),
        "norm_ffn_pre": 1.0 + nrm(ks[3], (L, D_MODEL), 0.02),
        "norm_ffn_post": 1.0 + nrm(ks[4], (L, D_MODEL), 0.02),
        "w_in": nrm(ks[5], (L, D_MODEL, D_IN), D_MODEL ** -0.5),
        "rel_bias": nrm(ks[6], (N_BUCKETS, N_Q_HEADS), 0.5),
        "sinks": nrm(ks[7], (L, N_Q_HEADS), 0.5),
        "rwkv_shift_mix": jax.random.uniform(ks[8], (L, RWKV_COLS), jnp.float32),
        "w0": jax.random.uniform(ks[9], (L, D_RWKV), jnp.float32, -5.0, 0.0),
        "w_decay_up": nrm(ks[10], (L, LORA_DECAY, D_RWKV), 0.5 * LORA_DECAY ** -0.5),
        "a0": nrm(ks[11], (L, D_RWKV), 0.1),
        "w_iclr_up": nrm(ks[12], (L, LORA_ICLR, D_RWKV), 0.5 * LORA_ICLR ** -0.5),
        "w_gate_up": nrm(ks[13], (L, LORA_GATE, D_RWKV), LORA_GATE ** -0.5),
        "k_k": 0.85 + nrm(ks[14], (L, D_RWKV), 0.05),
        "k_a": 1.0 + nrm(ks[15], (L, D_RWKV), 0.05),
        "r_k": nrm(ks[16], (L, N_RWKV_HEADS, HEAD_DIM), 0.1),
        "ln_x_g": 1.0 + nrm(ks[17], (L, D_RWKV), 0.02),
        "ln_x_b": nrm(ks[18], (L, D_RWKV), 0.02),
        "w_out": nrm(ks[19], (L, D_MODEL, D_MODEL), D_MODEL ** -0.5),
        "w_ffn_up": nrm(ks[20], (L, D_MODEL, 2 * D_FF), D_MODEL ** -0.5),
        "conv_w": nrm(ks[21], (L, CONV_WIDTH, 2 * D_FF), CONV_WIDTH ** -0.5),
        "conv_b": nrm(ks[22], (L, 2 * D_FF), 0.02),
        "w_ffn_down": nrm(ks[23], (L, D_FF, D_MODEL), D_FF ** -0.5),
    }


def reference(x, norm_mix_pre, norm_mix_post, norm_ffn_pre, norm_ffn_post, w_in, rel_bias, sinks,
              rwkv_shift_mix, w0, w_decay_up, a0, w_iclr_up, w_gate_up, k_k, k_a, r_k, ln_x_g, ln_x_b,
              w_out, w_ffn_up, conv_w, conv_b, w_ffn_down):
    for l in range(DEPTH):
        h = rms_norm(x, norm_mix_pre[l])
        proj = h @ w_in[l]
        q, k, v, p = jnp.split(proj, (D_ATTN, D_ATTN + D_KV, D_ATTN + 2 * D_KV), axis=-1)
        attn = sliding_window_sink_attention(q, k, v, rel_bias, sinks[l])
        p_prev = jnp.concatenate([jnp.zeros_like(p[:, :1]), p[:, :-1]], axis=1)
        p = p + (p_prev - p) * rwkv_shift_mix[l]
        rw = rwkv7_time_mix(p, w0[l], w_decay_up[l], a0[l], w_iclr_up[l], w_gate_up[l],
                            k_k[l], k_a[l], r_k[l], ln_x_g[l], ln_x_b[l])
        mix = jnp.concatenate([attn, rw.astype(x.dtype)], axis=-1) @ w_out[l]
        x = x + rms_norm(mix, norm_mix_post[l])
        f = conv_gated_ffn(rms_norm(x, norm_ffn_pre[l]), w_ffn_up[l], conv_w[l], conv_b[l], w_ffn_down[l])
        x = x + rms_norm(f, norm_ffn_post[l])
    return x
```

```python
import math

import jax
import jax.numpy as jnp
from jax import lax
from jax.experimental import pallas as pl
from jax.experimental.pallas import tpu as pltpu

D_MODEL = 1024
HEAD_DIM = 64
D_ATTN = 512
D_RWKV = 512
N_Q_HEADS = 8
N_KV_HEADS = 2
D_KV = 128
WINDOW = 128
BLOCK = 128
N_BUCKETS = 32
MAX_DISTANCE = 128
LORA_DECAY = 64
LORA_ICLR = 64
LORA_GATE = 128
RWKV_COLS = 3 * D_RWKV + LORA_DECAY + LORA_ICLR + LORA_GATE
D_FF = 4 * D_MODEL
NORM_EPS = 1e-6
GN_EPS = 64e-5
NEG_INF = -1e30

IN_PROJ_ROWS = 512
RWKV_ROWS = 512
CHUNK = 64
GROUP_LANES = 256
FFN_ROWS = 512
FFN_COLS = 512
VMEM_LIMIT = 56 * 1024 * 1024

_BF16 = jnp.bfloat16
_F32 = jnp.float32


def _dot(a, b):
    return jnp.dot(a.astype(_BF16), b.astype(_BF16), preferred_element_type=_F32)


def _dot_nt(a, b):
    return lax.dot_general(a.astype(_BF16), b.astype(_BF16), (((1,), (1,)), ((), ())),
                           preferred_element_type=_F32)


def _rms_norm(x, g):
    return x * lax.rsqrt(jnp.mean(x * x, axis=-1, keepdims=True) + NORM_EPS) * g


def _sigmoid(x):
    return 1.0 / (1.0 + jnp.exp(-x))


def _bias_kernel(rb_ref, bucket_ref, out_ref):
    h = pl.program_id(0)
    bucket = bucket_ref[...]
    qi = lax.broadcasted_iota(jnp.int32, (BLOCK, 2 * BLOCK), 0)
    kj = lax.broadcasted_iota(jnp.int32, (BLOCK, 2 * BLOCK), 1)
    rel = qi + BLOCK - kj
    acc = jnp.zeros((BLOCK, 2 * BLOCK), _F32)
    for b in range(N_BUCKETS):
        acc = jnp.where(bucket == b, rb_ref[b, h], acc)
    out_ref[0] = jnp.where((rel >= 0) & (rel < WINDOW), acc, NEG_INF)


def _bias_table(rel_bias, bucket):
    return pl.pallas_call(
        _bias_kernel,
        grid=(N_Q_HEADS,),
        in_specs=[
            pl.BlockSpec(memory_space=pltpu.SMEM),
            pl.BlockSpec((BLOCK, 2 * BLOCK), lambda h: (0, 0)),
        ],
        out_specs=pl.BlockSpec((1, BLOCK, 2 * BLOCK), lambda h: (h, 0, 0)),
        out_shape=jax.ShapeDtypeStruct((N_Q_HEADS, BLOCK, 2 * BLOCK), _F32),
        name="bias_table",
    )(rel_bias, bucket)


_QW = D_ATTN
_KW = 2 * D_KV
_IN_COLS = _QW + 2 * _KW + RWKV_COLS


def _in_proj_kernel(x_ref, g_ref, w_ref, q_ref, k_ref, v_ref, p_ref):
    h = _rms_norm(x_ref[...], g_ref[...]).astype(_BF16)
    q_ref[...] = (_dot(h, w_ref[:, 0:_QW]) * (HEAD_DIM ** -0.5)).astype(_BF16)
    k_ref[...] = _dot(h, w_ref[:, _QW:_QW + _KW]).astype(_BF16)
    v_ref[...] = _dot(h, w_ref[:, _QW + _KW:_QW + 2 * _KW]).astype(_BF16)
    p_ref[...] = _dot(h, w_ref[:, _QW + 2 * _KW:_IN_COLS])


def _in_proj(x2, g, w):
    t = x2.shape[0]
    tm = IN_PROJ_ROWS
    return pl.pallas_call(
        _in_proj_kernel,
        grid=(t // tm,),
        in_specs=[
            pl.BlockSpec((tm, D_MODEL), lambda i: (i, 0)),
            pl.BlockSpec((1, D_MODEL), lambda i: (0, 0)),
            pl.BlockSpec((D_MODEL, _IN_COLS), lambda i: (0, 0)),
        ],
        out_specs=[
            pl.BlockSpec((tm, _QW), lambda i: (i, 0)),
            pl.BlockSpec((tm, _KW), lambda i: (i, 0)),
            pl.BlockSpec((tm, _KW), lambda i: (i, 0)),
            pl.BlockSpec((tm, RWKV_COLS), lambda i: (i, 0)),
        ],
        out_shape=[
            jax.ShapeDtypeStruct((t, _QW), _BF16),
            jax.ShapeDtypeStruct((t, _KW), _BF16),
            jax.ShapeDtypeStruct((t, _KW), _BF16),
            jax.ShapeDtypeStruct((t, RWKV_COLS), _F32),
        ],
        compiler_params=pltpu.CompilerParams(
            dimension_semantics=("arbitrary",), vmem_limit_bytes=VMEM_LIMIT),
        name="in_proj",
    )(x2, g, w)


def _attn_kernel(sink_ref, q_ref, kc_ref, kp_ref, vc_ref, vp_ref, bias_ref, o_ref):
    n = pl.program_id(1)
    col = lax.broadcasted_iota(jnp.int32, (1, 2 * BLOCK), 1)
    valid = col >= jnp.where(n == 0, BLOCK, 0)
    lane = lax.broadcasted_iota(jnp.int32, (1, 2 * HEAD_DIM), 1)
    low = lane < HEAD_DIM
    for i in range(N_Q_HEADS // 2):
        j = (2 * i) // (N_Q_HEADS // N_KV_HEADS)
        lanes = slice(2 * HEAD_DIM * i, 2 * HEAD_DIM * (i + 1))
        kv_lanes = slice(2 * HEAD_DIM * j, 2 * HEAD_DIM * (j + 1))
        qp = q_ref[:, lanes]
        zero = jnp.zeros_like(qp)
        q_st = jnp.concatenate([jnp.where(low, qp, zero), jnp.where(low, zero, qp)], axis=0)
        kd = jnp.concatenate([kp_ref[:, kv_lanes], kc_ref[:, kv_lanes]], axis=0)
        vd = jnp.concatenate([vp_ref[:, kv_lanes], vc_ref[:, kv_lanes]], axis=0)
        s = _dot_nt(q_st, kd) + bias_ref[i]
        s = jnp.where(valid, s, NEG_INF)
        halves = []
        for hh in range(2):
            sh = s[BLOCK * hh:BLOCK * (hh + 1)]
            sink = sink_ref[2 * i + hh]
            m = jnp.maximum(jnp.max(sh, axis=-1, keepdims=True), sink)
            e = jnp.exp(sh - m)
            den = jnp.sum(e, axis=-1, keepdims=True) + jnp.exp(sink - m)
            halves.append(_dot(e, vd) / den)
        o_ref[:, lanes] = jnp.where(low, halves[0], halves[1]).astype(o_ref.dtype)


def _attention(sinks, q, kd, vd, bias, batch, seq):
    nb = seq // BLOCK
    cur = lambda b, n: (b * nb + n, 0)
    prev = lambda b, n: (jnp.maximum(b * nb + n - 1, 0), 0)
    return pl.pallas_call(
        _attn_kernel,
        grid=(batch, nb),
        in_specs=[
            pl.BlockSpec(memory_space=pltpu.SMEM),
            pl.BlockSpec((BLOCK, _QW), cur),
            pl.BlockSpec((BLOCK, _KW), cur),
            pl.BlockSpec((BLOCK, _KW), prev),
            pl.BlockSpec((BLOCK, _KW), cur),
            pl.BlockSpec((BLOCK, _KW), prev),
            pl.BlockSpec((N_Q_HEADS // 2, 2 * BLOCK, 2 * BLOCK), lambda b, n: (0, 0, 0)),
        ],
        out_specs=pl.BlockSpec((BLOCK, D_ATTN), cur),
        out_shape=jax.ShapeDtypeStruct((batch * seq, D_ATTN), _BF16),
        compiler_params=pltpu.CompilerParams(
            dimension_semantics=("arbitrary", "arbitrary"), vmem_limit_bytes=VMEM_LIMIT),
        name="attention",
    )(sinks, q, kd, kd, vd, vd, bias)


_HEADS_PER_GROUP = GROUP_LANES // HEAD_DIM
_N_GROUPS = D_RWKV // GROUP_LANES


def _split3(x):
    hi = x.astype(_BF16)
    r1 = x - hi.astype(_F32)
    mid = r1.astype(_BF16)
    lo = (r1 - mid.astype(_F32)).astype(_BF16)
    return hi, mid, lo


def _rwkv_kernel(p_ref, mix_ref, wl_ref, wg_ref, ones_ref, tri_ref, w0_ref, a0_ref, kk_ref, ka_ref,
                 rk_ref, lng_ref, lnb_ref, o_ref,
                 st_ref, carry_ref, r_s, kp_s, v_s, kn_s, be_s, lw_s, g_s, bo_s, o_s):
    tt = p_ref.shape[0]

    @pl.when(pl.program_id(1) == 0)
    def _():
        st_ref[...] = jnp.zeros_like(st_ref)
        carry_ref[...] = jnp.zeros_like(carry_ref)

    p = p_ref[...]
    rolled = pltpu.roll(p, 1, 0)
    row8 = lax.broadcasted_iota(jnp.int32, (8, 1), 0)
    head = jnp.where(row8 == 0, carry_ref[...], rolled[0:8])
    carry_ref[...] = rolled[0:8]
    p_prev = jnp.concatenate([head, rolled[8:]], axis=0)
    ps = p + (p_prev - p) * mix_ref[...]

    r = ps[:, 0:D_RWKV]
    k = ps[:, D_RWKV:2 * D_RWKV]
    v = ps[:, 2 * D_RWKV:3 * D_RWKV]
    z = ps[:, 3 * D_RWKV:3 * D_RWKV + LORA_DECAY + LORA_ICLR]
    zg = ps[:, 3 * D_RWKV + LORA_DECAY + LORA_ICLR:RWKV_COLS]

    lane128 = lax.broadcasted_iota(jnp.int32, (1, LORA_DECAY + LORA_ICLR), 1)
    zin = jnp.where(lane128 < LORA_DECAY, jnp.tanh(z), z)
    dl = _dot(zin, wl_ref[...])
    wpre = w0_ref[...] + dl[:, 0:D_RWKV]
    w_log = -(jnp.maximum(-wpre, 0.0) + jnp.log1p(jnp.exp(-jnp.abs(wpre)))) - 0.5
    lw = -jnp.exp(w_log)
    a = _sigmoid(a0_ref[...] + dl[:, D_RWKV:2 * D_RWKV])
    g_s[...] = _dot(_sigmoid(zg), wg_ref[...])

    ones_bd = ones_ref[...]
    kk = k * kk_ref[...]
    kk = kk / jnp.maximum(jnp.sqrt(_dot(kk * kk, ones_bd)), 1e-12)
    kp = k * (1.0 + (a - 1.0) * ka_ref[...])
    bo_s[...] = _dot(r * kp * rk_ref[...], ones_bd) * v
    r_s[...] = r
    kp_s[...] = kp
    v_s[...] = v
    kn_s[...] = kk
    be_s[...] = kk * a
    lw_s[...] = lw

    gl = GROUP_LANES
    lane = lax.broadcasted_iota(jnp.int32, (1, gl), 1)
    head_masks = [(lane >= HEAD_DIM * h) & (lane < HEAD_DIM * (h + 1)) for h in range(_HEADS_PER_GROUP)]
    s_local = lane & (HEAD_DIM - 1)
    trow = lax.broadcasted_iota(jnp.int32, (CHUNK, 1), 0)
    strict = s_local < trow
    incl = s_local <= trow
    eye_tok = jnp.where(s_local == trow, 1.0, 0.0).astype(_F32)
    grow = lax.broadcasted_iota(jnp.int32, (gl, 1), 0)
    same_head = (grow >> 6) == (lane >> 6)
    diag = grow == lane
    tri = tri_ref[...]

    def stack(x):
        xb = x.astype(_BF16)
        zero = jnp.zeros_like(xb)
        return jnp.concatenate([jnp.where(m, xb, zero) for m in head_masks], axis=0)

    def chunk_body(c, carry):
        rows = pl.ds(pl.multiple_of(c * CHUNK, CHUNK), CHUNK)
        lw_c = lw_s[rows, :]
        r_c = r_s[rows, :]
        kp_c = kp_s[rows, :]
        v_c = v_s[rows, :]
        kn_c = kn_s[rows, :]
        be_c = be_s[rows, :]
        hi, mid, lo = _split3(lw_c)
        cum = _dot(tri, hi) + _dot(tri, mid) + _dot(tri, lo)
        cum_last = cum[CHUNK - 1:CHUNK, :]
        q_t = r_c * jnp.exp(cum)
        al_t = -kn_c * jnp.exp(cum - lw_c)
        inv = jnp.exp(-cum)
        k_h = kp_c * inv
        b_h = be_c * inv
        rest = jnp.exp(cum_last - cum)
        k_b = kp_c * rest
        b_b = be_c * rest
        p_last = jnp.exp(cum_last)
        outs = []
        for gi in range(_N_GROUPS):
            sl = slice(gl * gi, gl * (gi + 1))
            al_g, q_g, v_g = al_t[:, sl], q_t[:, sl], v_c[:, sl]
            gram = _dot_nt(jnp.concatenate([al_g, q_g], axis=0),
                           jnp.concatenate([stack(k_h[:, sl]), stack(b_h[:, sl])], axis=0))
            a_ak = jnp.where(strict, gram[0:CHUNK, 0:gl], 0.0)
            a_ab = jnp.where(strict, gram[0:CHUNK, gl:2 * gl], 0.0)
            a_qk = jnp.where(incl, gram[CHUNK:2 * CHUNK, 0:gl], 0.0)
            a_qb = jnp.where(incl, gram[CHUNK:2 * CHUNK, gl:2 * gl], 0.0)
            npow = a_ab
            tinv = eye_tok + a_ab
            for _ in range(5):
                npow = _dot(npow, stack(npow))
                tinv = tinv + _dot(tinv, stack(npow))
            akv = _dot(a_ak, stack(v_g))
            y = _dot(tinv, jnp.concatenate([stack(al_g), stack(akv)], axis=1))
            al_hat, u_v = y[:, 0:gl], y[:, gl:2 * gl]
            q_hat = q_g + _dot(a_qb, stack(al_hat))
            o_intra = _dot(jnp.concatenate([a_qk, a_qb], axis=1),
                           jnp.concatenate([stack(v_g), stack(u_v)], axis=0))
            kb_t = jnp.concatenate([k_b[:, sl], b_b[:, sl]], axis=0).T
            upd = _dot(kb_t, jnp.concatenate(
                [jnp.concatenate([v_g, jnp.zeros_like(v_g)], axis=1),
                 jnp.concatenate([u_v, al_hat], axis=1)], axis=0))
            n_mat = jnp.where(same_head, upd[:, 0:gl], 0.0)
            m_mat = jnp.where(diag, p_last[:, sl], 0.0) + jnp.where(same_head, upd[:, gl:2 * gl], 0.0)
            st = st_ref[gi]
            outs.append(_dot(q_hat, st) + o_intra)
            m_hi = m_mat.astype(_BF16)
            m_lo = (m_mat - m_hi.astype(_F32)).astype(_BF16)
            s_hi = st.astype(_BF16)
            s_lo = (st - s_hi.astype(_F32)).astype(_BF16)
            st_ref[gi] = (_dot(m_hi, s_hi) + _dot(m_hi, s_lo) + _dot(m_lo, s_hi)) + n_mat
        o_s[rows, :] = jnp.concatenate(outs, axis=1)
        return carry

    lax.fori_loop(0, tt // CHUNK, chunk_body, 0)

    o = o_s[...]
    mu = _dot(o, ones_bd) * (1.0 / HEAD_DIM)
    d = o - mu
    var = _dot(d * d, ones_bd) * (1.0 / HEAD_DIM)
    on = d * lax.rsqrt(var + GN_EPS) * lng_ref[...] + lnb_ref[...]
    o_ref[...] = ((on + bo_s[...]) * g_s[...]).astype(o_ref.dtype)


def _rwkv(p, mix, wl, wg, ones_bd, tri, w0, a0, k_k, k_a, r_k, ln_g, ln_b, batch, seq):
    tt = RWKV_ROWS
    nt = seq // tt
    vec = lambda n: pl.BlockSpec((1, n), lambda b, t: (0, 0))
    full = lambda a: pl.BlockSpec(a.shape, lambda b, t: (0,) * a.ndim)
    tile = lambda: pltpu.VMEM((tt, D_RWKV), _F32)
    return pl.pallas_call(
        _rwkv_kernel,
        grid=(batch, nt),
        in_specs=[
            pl.BlockSpec((tt, RWKV_COLS), lambda b, t: (b * nt + t, 0)),
            vec(RWKV_COLS), full(wl), full(wg), full(ones_bd), full(tri),
            vec(D_RWKV), vec(D_RWKV), vec(D_RWKV), vec(D_RWKV), vec(D_RWKV), vec(D_RWKV), vec(D_RWKV),
        ],
        out_specs=pl.BlockSpec((tt, D_RWKV), lambda b, t: (b * nt + t, 0)),
        out_shape=jax.ShapeDtypeStruct((batch * seq, D_RWKV), _BF16),
        scratch_shapes=[
            pltpu.VMEM((_N_GROUPS, GROUP_LANES, GROUP_LANES), _F32),
            pltpu.VMEM((8, RWKV_COLS), _F32),
            tile(), tile(), tile(), tile(), tile(), tile(), tile(), tile(), tile(),
        ],
        compiler_params=pltpu.CompilerParams(
            dimension_semantics=("arbitrary", "arbitrary"), vmem_limit_bytes=VMEM_LIMIT),
        name="rwkv",
    )(p, mix, wl, wg, ones_bd, tri, w0, a0, k_k, k_a, r_k, ln_g, ln_b)


_N_FF_STEPS = D_FF // FFN_COLS


def _gelu_tanh(x):
    return 0.5 * x * (1.0 + jnp.tanh(0.7978845608028654 * (x + 0.044715 * (x * x * x))))


def _out_ffn_kernel(attn_ref, rw_ref, x_ref, wo_ref, gmp_ref, gfp_ref, gfo_ref,
                    wup_ref, cw_ref, cb_ref, wdn_ref, out_ref,
                    carry_ref, acc_ref, h_ref, x1_ref):
    @pl.when(pl.program_id(1) == 0)
    def _():
        carry_ref[...] = jnp.zeros_like(carry_ref)

    mix = _dot(attn_ref[...], wo_ref[0:D_ATTN, :]) + _dot(rw_ref[...], wo_ref[D_ATTN:D_MODEL, :])
    x1 = x_ref[...] + _rms_norm(mix, gmp_ref[...])
    x1_ref[...] = x1
    h_ref[...] = _rms_norm(x1, gfp_ref[...]).astype(_BF16)
    acc_ref[...] = jnp.zeros_like(acc_ref)
    row8 = lax.broadcasted_iota(jnp.int32, (8, 1), 0)

    def step(c, carry):
        u = jnp.dot(h_ref[...], wup_ref[c], preferred_element_type=_F32)
        r1 = pltpu.roll(u, 1, 0)
        r2 = pltpu.roll(u, 2, 0)
        cr = carry_ref[c]
        head1 = jnp.where(row8 < 1, cr[0:8], r1[0:8])
        head2 = jnp.where(row8 < 2, cr[8:16], r2[0:8])
        carry_ref[c] = jnp.concatenate([r1[0:8], r2[0:8]], axis=0)
        u1 = jnp.concatenate([head1, r1[8:]], axis=0)
        u2 = jnp.concatenate([head2, r2[8:]], axis=0)
        cw = cw_ref[c]
        uc = cb_ref[c] + cw[0:1] * u2 + cw[1:2] * u1 + cw[2:3] * u
        act = _gelu_tanh(uc[:, 0:FFN_COLS]) * uc[:, FFN_COLS:2 * FFN_COLS]
        acc_ref[...] += jnp.dot(act.astype(_BF16), wdn_ref[c], preferred_element_type=_F32)
        return carry

    lax.fori_loop(0, _N_FF_STEPS, step, 0)
    out_ref[...] = x1_ref[...] + _rms_norm(acc_ref[...], gfo_ref[...])


def _out_ffn(attn, rw, x2, wo, g_mix_post, g_ffn_pre, g_ffn_post, wup, cw, cb, wdn, batch, seq):
    tm = FFN_ROWS
    nt = seq // tm
    rows = lambda n: pl.BlockSpec((tm, n), lambda b, t: (b * nt + t, 0))
    vec = pl.BlockSpec((1, D_MODEL), lambda b, t: (0, 0))
    const = lambda a: pl.BlockSpec(a.shape, lambda b, t: (0,) * a.ndim, pipeline_mode=pl.Buffered(1))
    return pl.pallas_call(
        _out_ffn_kernel,
        grid=(batch, nt),
        in_specs=[rows(D_ATTN), rows(D_RWKV), rows(D_MODEL), const(wo), vec, vec, vec,
                  const(wup), const(cw), const(cb), const(wdn)],
        out_specs=rows(D_MODEL),
        out_shape=jax.ShapeDtypeStruct((batch * seq, D_MODEL), _F32),
        scratch_shapes=[
            pltpu.VMEM((_N_FF_STEPS, 16, 2 * FFN_COLS), _F32),
            pltpu.VMEM((tm, D_MODEL), _F32),
            pltpu.VMEM((tm, D_MODEL), _BF16),
            pltpu.VMEM((tm, D_MODEL), _F32),
        ],
        compiler_params=pltpu.CompilerParams(
            dimension_semantics=("arbitrary", "arbitrary"), vmem_limit_bytes=VMEM_LIMIT),
        name="out_ffn",
    )(attn, rw, x2, wo, g_mix_post, g_ffn_pre, g_ffn_post, wup, cw, cb, wdn)


def _t5_bucket(rel):
    n = jnp.maximum(rel, 0)
    max_exact = N_BUCKETS // 2
    large = max_exact + (jnp.log(jnp.maximum(n, 1).astype(_F32) / max_exact)
                         / math.log(MAX_DISTANCE / max_exact)
                         * (N_BUCKETS - max_exact)).astype(jnp.int32)
    large = jnp.minimum(large, N_BUCKETS - 1)
    return jnp.where(n < max_exact, n, large)


def _layer(x2, batch, seq, norm_mix_pre, norm_mix_post, norm_ffn_pre, norm_ffn_post, w_in, bias, sinks,
           shift_mix, w0, w_decay_up, a0, w_iclr_up, w_gate_up, k_k, k_a, r_k, ln_x_g, ln_x_b,
           w_out, w_ffn_up, conv_w, conv_b, w_ffn_down):
    row = lambda a: a.reshape(1, -1)
    wq = w_in[:, 0:D_ATTN]
    wk = w_in[:, D_ATTN:D_ATTN + D_KV]
    wv = w_in[:, D_ATTN + D_KV:D_ATTN + 2 * D_KV]
    wp = w_in[:, D_ATTN + 2 * D_KV:]
    dup = lambda w: jnp.concatenate(
        [w[:, HEAD_DIM * (j // 2):HEAD_DIM * (j // 2 + 1)] for j in range(2 * N_KV_HEADS)], axis=1)
    w_cat = jnp.concatenate([wq, dup(wk), dup(wv), wp], axis=1).astype(_BF16)
    q, kd, vd, p = _in_proj(x2, row(norm_mix_pre), w_cat)

    attn = _attention(sinks, q, kd, vd, bias, batch, seq)

    zeros = jnp.zeros((LORA_DECAY, D_RWKV), _F32)
    w_lora = jnp.concatenate([jnp.concatenate([w_decay_up, zeros], axis=1),
                              jnp.concatenate([zeros, w_iclr_up], axis=1)], axis=0).astype(_BF16)
    hid = jnp.arange(D_RWKV) // HEAD_DIM
    ones_bd = (hid[:, None] == hid[None, :]).astype(_BF16)
    tri = (jnp.arange(CHUNK)[:, None] >= jnp.arange(CHUNK)[None, :]).astype(_BF16)
    rw = _rwkv(p, row(shift_mix), w_lora, w_gate_up.astype(_BF16), ones_bd, tri, row(w0), row(a0), row(k_k),
               row(k_a), row(r_k), row(ln_x_g), row(ln_x_b), batch, seq)

    nc, tf = _N_FF_STEPS, FFN_COLS
    pair = lambda a: jnp.concatenate(
        [a[..., 0:D_FF].reshape(a.shape[:-1] + (nc, tf)), a[..., D_FF:].reshape(a.shape[:-1] + (nc, tf))], axis=-1)
    wup = pair(w_ffn_up).transpose(1, 0, 2).astype(_BF16)
    cw = jnp.pad(pair(conv_w).transpose(1, 0, 2), ((0, 0), (0, 5), (0, 0)))
    cb = pair(conv_b.reshape(1, -1)).transpose(1, 0, 2)
    wdn = w_ffn_down.reshape(nc, tf, D_MODEL).astype(_BF16)
    return _out_ffn(attn, rw, x2, w_out.astype(_BF16), row(norm_mix_post), row(norm_ffn_pre), row(norm_ffn_post),
                    wup, cw, cb, wdn, batch, seq)


def kernel(x, norm_mix_pre, norm_mix_post, norm_ffn_pre, norm_ffn_post, w_in, rel_bias, sinks, rwkv_shift_mix, w0, w_decay_up, a0, w_iclr_up, w_gate_up, k_k, k_a, r_k, ln_x_g, ln_x_b, w_out, w_ffn_up, conv_w, conv_b, w_ffn_down):
    batch, seq, _ = x.shape
    depth = w_in.shape[0]
    rel = (jnp.arange(BLOCK)[:, None] + BLOCK) - jnp.arange(2 * BLOCK)[None, :]
    bias = _bias_table(rel_bias.astype(_F32), _t5_bucket(rel).astype(jnp.int32))
    bias = bias.reshape(N_Q_HEADS // 2, 2 * BLOCK, 2 * BLOCK)
    x2 = x.reshape(batch * seq, D_MODEL)
    for l in range(depth):
        x2 = _layer(x2, batch, seq, norm_mix_pre[l], norm_mix_post[l], norm_ffn_pre[l], norm_ffn_post[l], w_in[l],
                    bias, sinks[l], rwkv_shift_mix[l], w0[l], w_decay_up[l], a0[l], w_iclr_up[l], w_gate_up[l],
                    k_k[l], k_a[l], r_k[l].reshape(-1), ln_x_g[l], ln_x_b[l], w_out[l], w_ffn_up[l], conv_w[l],
                    conv_b[l], w_ffn_down[l])
    return x2.reshape(batch, seq, D_MODEL)
```

```python
import math

import jax
import jax.numpy as jnp
from jax import lax
from jax.experimental import pallas as pl
from jax.experimental.pallas import tpu as pltpu

D_MODEL = 1024
HEAD_DIM = 64
D_ATTN = 512
D_RWKV = 512
N_Q_HEADS = 8
N_KV_HEADS = 2
D_KV = 128
WINDOW = 128
BLOCK = 128
N_BUCKETS = 32
MAX_DISTANCE = 128
LORA_DECAY = 64
LORA_ICLR = 64
LORA_GATE = 128
RWKV_COLS = 3 * D_RWKV + LORA_DECAY + LORA_ICLR + LORA_GATE
D_FF = 4 * D_MODEL
NORM_EPS = 1e-6
GN_EPS = 64e-5
NEG_INF = -1e30

IN_PROJ_ROWS = 512
RWKV_ROWS = 512
CHUNK = 64
GROUP_LANES = 256
FFN_ROWS = 512
FFN_COLS = 512
VMEM_LIMIT = 56 * 1024 * 1024

_BF16 = jnp.bfloat16
_F32 = jnp.float32


def _dot(a, b):
    return jnp.dot(a.astype(_BF16), b.astype(_BF16), preferred_element_type=_F32)


def _dot_nt(a, b):
    return lax.dot_general(a.astype(_BF16), b.astype(_BF16), (((1,), (1,)), ((), ())),
                           preferred_element_type=_F32)


def _rms_norm(x, g):
    return x * lax.rsqrt(jnp.mean(x * x, axis=-1, keepdims=True) + NORM_EPS) * g


def _sigmoid(x):
    return 1.0 / (1.0 + jnp.exp(-x))


def _bias_kernel(rb_ref, bucket_ref, out_ref):
    h = pl.program_id(0)
    bucket = bucket_ref[...]
    qi = lax.broadcasted_iota(jnp.int32, (BLOCK, 2 * BLOCK), 0)
    kj = lax.broadcasted_iota(jnp.int32, (BLOCK, 2 * BLOCK), 1)
    rel = qi + BLOCK - kj
    acc = jnp.zeros((BLOCK, 2 * BLOCK), _F32)
    for b in range(N_BUCKETS):
        acc = jnp.where(bucket == b, rb_ref[b, h], acc)
    out_ref[0] = jnp.where((rel >= 0) & (rel < WINDOW), acc, NEG_INF)


def _bias_table(rel_bias, bucket):
    return pl.pallas_call(
        _bias_kernel,
        grid=(N_Q_HEADS,),
        in_specs=[
            pl.BlockSpec(memory_space=pltpu.SMEM),
            pl.BlockSpec((BLOCK, 2 * BLOCK), lambda h: (0, 0)),
        ],
        out_specs=pl.BlockSpec((1, BLOCK, 2 * BLOCK), lambda h: (h, 0, 0)),
        out_shape=jax.ShapeDtypeStruct((N_Q_HEADS, BLOCK, 2 * BLOCK), _F32),
        name="bias_table",
    )(rel_bias, bucket)


_QW = D_ATTN
_KW = 2 * D_KV
_IN_COLS = _QW + 2 * _KW + RWKV_COLS


def _in_proj_kernel(x_ref, g_ref, w_ref, q_ref, k_ref, v_ref, p_ref):
    h = _rms_norm(x_ref[...], g_ref[...]).astype(_BF16)
    q_ref[...] = (_dot(h, w_ref[:, 0:_QW]) * (HEAD_DIM ** -0.5)).astype(_BF16)
    k_ref[...] = _dot(h, w_ref[:, _QW:_QW + _KW]).astype(_BF16)
    v_ref[...] = _dot(h, w_ref[:, _QW + _KW:_QW + 2 * _KW]).astype(_BF16)
    p_ref[...] = _dot(h, w_ref[:, _QW + 2 * _KW:_IN_COLS])


def _in_proj(x2, g, w):
    t = x2.shape[0]
    tm = IN_PROJ_ROWS
    return pl.pallas_call(
        _in_proj_kernel,
        grid=(t // tm,),
        in_specs=[
            pl.BlockSpec((tm, D_MODEL), lambda i: (i, 0)),
            pl.BlockSpec((1, D_MODEL), lambda i: (0, 0)),
            pl.BlockSpec((D_MODEL, _IN_COLS), lambda i: (0, 0)),
        ],
        out_specs=[
            pl.BlockSpec((tm, _QW), lambda i: (i, 0)),
            pl.BlockSpec((tm, _KW), lambda i: (i, 0)),
            pl.BlockSpec((tm, _KW), lambda i: (i, 0)),
            pl.BlockSpec((tm, RWKV_COLS), lambda i: (i, 0)),
        ],
        out_shape=[
            jax.ShapeDtypeStruct((t, _QW), _BF16),
            jax.ShapeDtypeStruct((t, _KW), _BF16),
            jax.ShapeDtypeStruct((t, _KW), _BF16),
            jax.ShapeDtypeStruct((t, RWKV_COLS), _F32),
        ],
        compiler_params=pltpu.CompilerParams(
            dimension_semantics=("arbitrary",), vmem_limit_bytes=VMEM_LIMIT),
        name="in_proj",
    )(x2, g, w)


def _attn_kernel(sink_ref, q_ref, kc_ref, kp_ref, vc_ref, vp_ref, bias_ref, o_ref):
    n = pl.program_id(1)
    col = lax.broadcasted_iota(jnp.int32, (1, 2 * BLOCK), 1)
    valid = col >= jnp.where(n == 0, BLOCK, 0)
    lane = lax.broadcasted_iota(jnp.int32, (1, 2 * HEAD_DIM), 1)
    low = lane < HEAD_DIM
    for i in range(N_Q_HEADS // 2):
        j = (2 * i) // (N_Q_HEADS // N_KV_HEADS)
        lanes = slice(2 * HEAD_DIM * i, 2 * HEAD_DIM * (i + 1))
        kv_lanes = slice(2 * HEAD_DIM * j, 2 * HEAD_DIM * (j + 1))
        qp = q_ref[:, lanes]
        zero = jnp.zeros_like(qp)
        q_st = jnp.concatenate([jnp.where(low, qp, zero), jnp.where(low, zero, qp)], axis=0)
        kd = jnp.concatenate([kp_ref[:, kv_lanes], kc_ref[:, kv_lanes]], axis=0)
        vd = jnp.concatenate([vp_ref[:, kv_lanes], vc_ref[:, kv_lanes]], axis=0)
        s = _dot_nt(q_st, kd) + bias_ref[i]
        s = jnp.where(valid, s, NEG_INF)
        halves = []
        for hh in range(2):
            sh = s[BLOCK * hh:BLOCK * (hh + 1)]
            sink = sink_ref[2 * i + hh]
            m = jnp.maximum(jnp.max(sh, axis=-1, keepdims=True), sink)
            e = jnp.exp(sh - m)
            den = jnp.sum(e, axis=-1, keepdims=True) + jnp.exp(sink - m)
            halves.append(_dot(e, vd) / den)
        o_ref[:, lanes] = jnp.where(low, halves[0], halves[1]).astype(o_ref.dtype)


def _attention(sinks, q, kd, vd, bias, batch, seq):
    nb = seq // BLOCK
    cur = lambda b, n: (b * nb + n, 0)
    prev = lambda b, n: (jnp.maximum(b * nb + n - 1, 0), 0)
    return pl.pallas_call(
        _attn_kernel,
        grid=(batch, nb),
        in_specs=[
            pl.BlockSpec(memory_space=pltpu.SMEM),
            pl.BlockSpec((BLOCK, _QW), cur),
            pl.BlockSpec((BLOCK, _KW), cur),
            pl.BlockSpec((BLOCK, _KW), prev),
            pl.BlockSpec((BLOCK, _KW), cur),
            pl.BlockSpec((BLOCK, _KW), prev),
            pl.BlockSpec((N_Q_HEADS // 2, 2 * BLOCK, 2 * BLOCK), lambda b, n: (0, 0, 0)),
        ],
        out_specs=pl.BlockSpec((BLOCK, D_ATTN), cur),
        out_shape=jax.ShapeDtypeStruct((batch * seq, D_ATTN), _BF16),
        compiler_params=pltpu.CompilerParams(
            dimension_semantics=("arbitrary", "arbitrary"), vmem_limit_bytes=VMEM_LIMIT),
        name="attention",
    )(sinks, q, kd, kd, vd, vd, bias)


_HEADS_PER_GROUP = GROUP_LANES // HEAD_DIM
_N_GROUPS = D_RWKV // GROUP_LANES
_CHUNKS_PER_STEP = 4


def _split3(x):
    hi = x.astype(_BF16)
    r1 = x - hi.astype(_F32)
    mid = r1.astype(_BF16)
    lo = (r1 - mid.astype(_F32)).astype(_BF16)
    return hi, mid, lo


def _rwkv_kernel(p_ref, mix_ref, wl_ref, wg_ref, ones_ref, tri_ref, w0_ref, a0_ref, kk_ref, ka_ref,
                 rk_ref, lng_ref, lnb_ref, o_ref,
                 st_ref, carry_ref, q_s, al_s, kh_s, bh_s, kb_s, bb_s, v_s, plast_s,
                 qh_s, oi_s, mhi_s, mlo_s, n_s, g_s, bo_s, o_s):
    tt = p_ref.shape[0]
    n_chunks = tt // CHUNK

    @pl.when(pl.program_id(1) == 0)
    def _():
        st_ref[...] = jnp.zeros_like(st_ref)
        carry_ref[...] = jnp.zeros_like(carry_ref)

    p = p_ref[...]
    rolled = pltpu.roll(p, 1, 0)
    row8 = lax.broadcasted_iota(jnp.int32, (8, 1), 0)
    head = jnp.where(row8 == 0, carry_ref[...], rolled[0:8])
    carry_ref[...] = rolled[0:8]
    p_prev = jnp.concatenate([head, rolled[8:]], axis=0)
    ps = p + (p_prev - p) * mix_ref[...]

    r = ps[:, 0:D_RWKV]
    k = ps[:, D_RWKV:2 * D_RWKV]
    v = ps[:, 2 * D_RWKV:3 * D_RWKV]
    z = ps[:, 3 * D_RWKV:3 * D_RWKV + LORA_DECAY + LORA_ICLR]
    zg = ps[:, 3 * D_RWKV + LORA_DECAY + LORA_ICLR:RWKV_COLS]

    lane128 = lax.broadcasted_iota(jnp.int32, (1, LORA_DECAY + LORA_ICLR), 1)
    zin = jnp.where(lane128 < LORA_DECAY, jnp.tanh(z), z)
    dl = _dot(zin, wl_ref[...])
    wpre = w0_ref[...] + dl[:, 0:D_RWKV]
    lw = (-math.exp(-0.5)) * _sigmoid(wpre)
    a = _sigmoid(a0_ref[...] + dl[:, D_RWKV:2 * D_RWKV])
    g_s[...] = _dot(_sigmoid(zg), wg_ref[...])

    ones_bd = ones_ref[...]
    kk = k * kk_ref[...]
    kk = kk * lax.rsqrt(jnp.maximum(_dot(kk * kk, ones_bd), 1e-24))
    kp = k * (1.0 + (a - 1.0) * ka_ref[...])
    be = kk * a
    bo_s[...] = _dot(r * kp * rk_ref[...], ones_bd) * v
    v_s[...] = v.astype(_BF16)

    tri = tri_ref[...]
    cums, lasts = [], []
    for c in range(n_chunks):
        hi, mid, lo = _split3(lw[c * CHUNK:(c + 1) * CHUNK])
        cum_c = _dot(tri, hi) + _dot(tri, mid) + _dot(tri, lo)
        last = cum_c[CHUNK - 1:CHUNK, :]
        plast_s[c:c + 1, :] = jnp.exp(last)
        cums.append(cum_c)
        lasts.append(jnp.broadcast_to(last, (CHUNK, D_RWKV)))
    cum = jnp.concatenate(cums, axis=0)
    cum_last = jnp.concatenate(lasts, axis=0)
    q_s[...] = (r * jnp.exp(cum)).astype(_BF16)
    al_s[...] = (-kk * jnp.exp(cum - lw)).astype(_BF16)
    inv = jnp.exp(-cum)
    kh_s[...] = (kp * inv).astype(_BF16)
    bh_s[...] = (be * inv).astype(_BF16)
    rest = jnp.exp(cum_last - cum)
    kb_s[...] = (kp * rest).astype(_BF16)
    bb_s[...] = (be * rest).astype(_BF16)

    gl = GROUP_LANES
    lane = lax.broadcasted_iota(jnp.int32, (1, gl), 1)
    head_masks = [(lane >= HEAD_DIM * h) & (lane < HEAD_DIM * (h + 1)) for h in range(_HEADS_PER_GROUP)]
    s_local = lane & (HEAD_DIM - 1)
    trow = lax.broadcasted_iota(jnp.int32, (CHUNK, 1), 0)
    strict = s_local < trow
    incl = s_local <= trow
    eye_tok = jnp.where(s_local == trow, 1.0, 0.0).astype(_F32)
    grow = lax.broadcasted_iota(jnp.int32, (gl, 1), 0)
    same_head = (grow >> 6) == (lane >> 6)
    diag = grow == lane

    def stack(x):
        xb = x.astype(_BF16)
        zero = jnp.zeros_like(xb)
        return jnp.concatenate([jnp.where(m, xb, zero) for m in head_masks], axis=0)

    def intra_body(it, carry):
        chains = [(it * _CHUNKS_PER_STEP + u, gi) for u in range(_CHUNKS_PER_STEP) for gi in range(_N_GROUPS)]
        idx = [(pl.ds(pl.multiple_of(c * CHUNK, CHUNK), CHUNK), slice(gl * gi, gl * (gi + 1))) for c, gi in chains]
        al = [al_s[r, s] for r, s in idx]
        q = [q_s[r, s] for r, s in idx]
        v = [v_s[r, s] for r, s in idx]
        gram = [_dot_nt(jnp.concatenate([a, b], axis=0),
                        jnp.concatenate([stack(kh_s[r, s]), stack(bh_s[r, s])], axis=0))
                for a, b, (r, s) in zip(al, q, idx)]
        a_ak = [jnp.where(strict, g[0:CHUNK, 0:gl], 0.0) for g in gram]
        a_ab = [jnp.where(strict, g[0:CHUNK, gl:2 * gl], 0.0) for g in gram]
        a_qk = [jnp.where(incl, g[CHUNK:2 * CHUNK, 0:gl], 0.0) for g in gram]
        a_qb = [jnp.where(incl, g[CHUNK:2 * CHUNK, gl:2 * gl], 0.0) for g in gram]
        akv = [_dot(a, stack(b)) for a, b in zip(a_ak, v)]
        pw = [_dot(n, stack(n)) for n in a_ab]
        tinv = [eye_tok + n for n in a_ab]
        for level in range(1, 6):
            rhs = [stack(x) for x in pw]
            if level < 5:
                res = [_dot(jnp.concatenate([x, t], axis=0), w) for x, t, w in zip(pw, tinv, rhs)]
                pw = [x[0:CHUNK] for x in res]
                tinv = [t + x[CHUNK:2 * CHUNK] for t, x in zip(tinv, res)]
            else:
                tinv = [t + _dot(t, w) for t, w in zip(tinv, rhs)]
        y = [_dot(t, jnp.concatenate([stack(a), stack(b)], axis=1)) for t, a, b in zip(tinv, al, akv)]
        al_hat = [x[:, 0:gl] for x in y]
        u_v = [x[:, gl:2 * gl] for x in y]
        qh = [_dot(a, stack(b)) for a, b in zip(a_qb, al_hat)]
        oi = [_dot(jnp.concatenate([a, b], axis=1), jnp.concatenate([stack(c_), stack(d)], axis=0))
              for a, b, c_, d in zip(a_qk, a_qb, v, u_v)]
        upd = [_dot(jnp.concatenate([kb_s[r, s], bb_s[r, s]], axis=0).astype(_F32).T,
                    jnp.concatenate([jnp.concatenate([c_, jnp.zeros_like(c_)], axis=1),
                                     jnp.concatenate([d, e], axis=1).astype(_BF16)], axis=0))
               for (r, s), c_, d, e in zip(idx, v, u_v, al_hat)]
        for (c, gi), (r, s), q_i, qh_i, oi_i, upd_i in zip(chains, idx, q, qh, oi, upd):
            qh_s[r, s] = (q_i.astype(_F32) + qh_i).astype(_BF16)
            oi_s[r, s] = oi_i
            n_s[c, gi] = jnp.where(same_head, upd_i[:, 0:gl], 0.0)
            p_last = plast_s[pl.ds(c, 1), s]
            m_mat = jnp.where(diag, p_last, 0.0) + jnp.where(same_head, upd_i[:, gl:2 * gl], 0.0)
            m_hi = m_mat.astype(_BF16)
            mhi_s[c, gi] = m_hi
            mlo_s[c, gi] = (m_mat - m_hi.astype(_F32)).astype(_BF16)
        return carry

    lax.fori_loop(0, n_chunks // _CHUNKS_PER_STEP, intra_body, 0)

    def state_body(c, carry):
        rows = pl.ds(pl.multiple_of(c * CHUNK, CHUNK), CHUNK)
        sls = [slice(gl * gi, gl * (gi + 1)) for gi in range(_N_GROUPS)]
        st = [st_ref[gi] for gi in range(_N_GROUPS)]
        s_hi = [x.astype(_BF16) for x in st]
        s_lo = [(x - h.astype(_F32)).astype(_BF16) for x, h in zip(st, s_hi)]
        new = [_dot(jnp.concatenate([mhi_s[c, gi], mhi_s[c, gi], mlo_s[c, gi]], axis=1),
                    jnp.concatenate([s_hi[gi], s_lo[gi], s_hi[gi]], axis=0)) for gi in range(_N_GROUPS)]
        out = [_dot(qh_s[rows, sls[gi]], s_hi[gi]) for gi in range(_N_GROUPS)]
        for gi in range(_N_GROUPS):
            st_ref[gi] = new[gi] + n_s[c, gi]
            o_s[rows, sls[gi]] = out[gi] + oi_s[rows, sls[gi]]
        return carry

    lax.fori_loop(0, n_chunks, state_body, 0)

    o = o_s[...]
    mu = _dot(o, ones_bd) * (1.0 / HEAD_DIM)
    d = o - mu
    var = _dot(d * d, ones_bd) * (1.0 / HEAD_DIM)
    on = d * lax.rsqrt(var + GN_EPS) * lng_ref[...] + lnb_ref[...]
    o_ref[...] = ((on + bo_s[...]) * g_s[...]).astype(o_ref.dtype)


def _rwkv(p, mix, wl, wg, ones_bd, tri, w0, a0, k_k, k_a, r_k, ln_g, ln_b, batch, seq):
    tt = RWKV_ROWS
    nt = seq // tt
    vec = lambda n: pl.BlockSpec((1, n), lambda b, t: (0, 0))
    full = lambda a: pl.BlockSpec(a.shape, lambda b, t: (0,) * a.ndim)
    tile = lambda dt: pltpu.VMEM((tt, D_RWKV), dt)
    per_chunk = lambda dt: pltpu.VMEM((tt // CHUNK, _N_GROUPS, GROUP_LANES, GROUP_LANES), dt)
    return pl.pallas_call(
        _rwkv_kernel,
        grid=(batch, nt),
        in_specs=[
            pl.BlockSpec((tt, RWKV_COLS), lambda b, t: (b * nt + t, 0)),
            vec(RWKV_COLS), full(wl), full(wg), full(ones_bd), full(tri),
            vec(D_RWKV), vec(D_RWKV), vec(D_RWKV), vec(D_RWKV), vec(D_RWKV), vec(D_RWKV), vec(D_RWKV),
        ],
        out_specs=pl.BlockSpec((tt, D_RWKV), lambda b, t: (b * nt + t, 0)),
        out_shape=jax.ShapeDtypeStruct((batch * seq, D_RWKV), _BF16),
        scratch_shapes=[
            pltpu.VMEM((_N_GROUPS, GROUP_LANES, GROUP_LANES), _F32),
            pltpu.VMEM((8, RWKV_COLS), _F32),
            tile(_BF16), tile(_BF16), tile(_BF16), tile(_BF16), tile(_BF16), tile(_BF16), tile(_BF16),
            pltpu.VMEM((tt // CHUNK, D_RWKV), _F32),
            tile(_BF16), tile(_F32), per_chunk(_BF16), per_chunk(_BF16), per_chunk(_F32),
            tile(_F32), tile(_F32), tile(_F32),
        ],
        compiler_params=pltpu.CompilerParams(
            dimension_semantics=("arbitrary", "arbitrary"), vmem_limit_bytes=VMEM_LIMIT),
        name="rwkv",
    )(p, mix, wl, wg, ones_bd, tri, w0, a0, k_k, k_a, r_k, ln_g, ln_b)


_N_FF_STEPS = D_FF // FFN_COLS


def _gelu_tanh(x):
    return 0.5 * x * (1.0 + jnp.tanh(0.7978845608028654 * (x + 0.044715 * (x * x * x))))


def _out_ffn_kernel(attn_ref, rw_ref, x_ref, wo_ref, gmp_ref, gfp_ref, gfo_ref,
                    wup_ref, cw_ref, cb_ref, wdn_ref, out_ref,
                    carry_ref, acc_ref, h_ref, x1_ref):
    @pl.when(pl.program_id(1) == 0)
    def _():
        carry_ref[...] = jnp.zeros_like(carry_ref)

    mix = _dot(attn_ref[...], wo_ref[0:D_ATTN, :]) + _dot(rw_ref[...], wo_ref[D_ATTN:D_MODEL, :])
    x1 = x_ref[...] + _rms_norm(mix, gmp_ref[...])
    x1_ref[...] = x1
    h_ref[...] = _rms_norm(x1, gfp_ref[...]).astype(_BF16)
    acc_ref[...] = jnp.zeros_like(acc_ref)
    row8 = lax.broadcasted_iota(jnp.int32, (8, 1), 0)

    def step(c, carry):
        u = jnp.dot(h_ref[...], wup_ref[c], preferred_element_type=_F32)
        r1 = pltpu.roll(u, 1, 0)
        r2 = pltpu.roll(u, 2, 0)
        cr = carry_ref[c]
        head1 = jnp.where(row8 < 1, cr[0:8], r1[0:8])
        head2 = jnp.where(row8 < 2, cr[8:16], r2[0:8])
        carry_ref[c] = jnp.concatenate([r1[0:8], r2[0:8]], axis=0)
        u1 = jnp.concatenate([head1, r1[8:]], axis=0)
        u2 = jnp.concatenate([head2, r2[8:]], axis=0)
        cw = cw_ref[c]
        uc = cb_ref[c] + cw[0:1] * u2 + cw[1:2] * u1 + cw[2:3] * u
        act = _gelu_tanh(uc[:, 0:FFN_COLS]) * uc[:, FFN_COLS:2 * FFN_COLS]
        acc_ref[...] += jnp.dot(act.astype(_BF16), wdn_ref[c], preferred_element_type=_F32)
        return carry

    lax.fori_loop(0, _N_FF_STEPS, step, 0)
    out_ref[...] = x1_ref[...] + _rms_norm(acc_ref[...], gfo_ref[...])


def _out_ffn(attn, rw, x2, wo, g_mix_post, g_ffn_pre, g_ffn_post, wup, cw, cb, wdn, batch, seq):
    tm = FFN_ROWS
    nt = seq // tm
    rows = lambda n: pl.BlockSpec((tm, n), lambda b, t: (b * nt + t, 0))
    vec = pl.BlockSpec((1, D_MODEL), lambda b, t: (0, 0))
    const = lambda a: pl.BlockSpec(a.shape, lambda b, t: (0,) * a.ndim, pipeline_mode=pl.Buffered(1))
    return pl.pallas_call(
        _out_ffn_kernel,
        grid=(batch, nt),
        in_specs=[rows(D_ATTN), rows(D_RWKV), rows(D_MODEL), const(wo), vec, vec, vec,
                  const(wup), const(cw), const(cb), const(wdn)],
        out_specs=rows(D_MODEL),
        out_shape=jax.ShapeDtypeStruct((batch * seq, D_MODEL), _F32),
        scratch_shapes=[
            pltpu.VMEM((_N_FF_STEPS, 16, 2 * FFN_COLS), _F32),
            pltpu.VMEM((tm, D_MODEL), _F32),
            pltpu.VMEM((tm, D_MODEL), _BF16),
            pltpu.VMEM((tm, D_MODEL), _F32),
        ],
        compiler_params=pltpu.CompilerParams(
            dimension_semantics=("arbitrary", "arbitrary"), vmem_limit_bytes=VMEM_LIMIT),
        name="out_ffn",
    )(attn, rw, x2, wo, g_mix_post, g_ffn_pre, g_ffn_post, wup, cw, cb, wdn)


def _t5_bucket(rel):
    n = jnp.maximum(rel, 0)
    max_exact = N_BUCKETS // 2
    large = max_exact + (jnp.log(jnp.maximum(n, 1).astype(_F32) / max_exact)
                         / math.log(MAX_DISTANCE / max_exact)
                         * (N_BUCKETS - max_exact)).astype(jnp.int32)
    large = jnp.minimum(large, N_BUCKETS - 1)
    return jnp.where(n < max_exact, n, large)


def _layer(x2, batch, seq, norm_mix_pre, norm_mix_post, norm_ffn_pre, norm_ffn_post, w_in, bias, sinks,
           shift_mix, w0, w_decay_up, a0, w_iclr_up, w_gate_up, k_k, k_a, r_k, ln_x_g, ln_x_b,
           w_out, w_ffn_up, conv_w, conv_b, w_ffn_down):
    row = lambda a: a.reshape(1, -1)
    wq = w_in[:, 0:D_ATTN]
    wk = w_in[:, D_ATTN:D_ATTN + D_KV]
    wv = w_in[:, D_ATTN + D_KV:D_ATTN + 2 * D_KV]
    wp = w_in[:, D_ATTN + 2 * D_KV:]
    dup = lambda w: jnp.concatenate(
        [w[:, HEAD_DIM * (j // 2):HEAD_DIM * (j // 2 + 1)] for j in range(2 * N_KV_HEADS)], axis=1)
    w_cat = jnp.concatenate([wq, dup(wk), dup(wv), wp], axis=1).astype(_BF16)
    q, kd, vd, p = _in_proj(x2, row(norm_mix_pre), w_cat)

    attn = _attention(sinks, q, kd, vd, bias, batch, seq)

    zeros = jnp.zeros((LORA_DECAY, D_RWKV), _F32)
    w_lora = jnp.concatenate([jnp.concatenate([w_decay_up, zeros], axis=1),
                              jnp.concatenate([zeros, w_iclr_up], axis=1)], axis=0).astype(_BF16)
    hid = jnp.arange(D_RWKV) // HEAD_DIM
    ones_bd = (hid[:, None] == hid[None, :]).astype(_BF16)
    tri = (jnp.arange(CHUNK)[:, None] >= jnp.arange(CHUNK)[None, :]).astype(_BF16)
    rw = _rwkv(p, row(shift_mix), w_lora, w_gate_up.astype(_BF16), ones_bd, tri, row(w0), row(a0), row(k_k),
               row(k_a), row(r_k), row(ln_x_g), row(ln_x_b), batch, seq)

    nc, tf = _N_FF_STEPS, FFN_COLS
    pair = lambda a: jnp.concatenate(
        [a[..., 0:D_FF].reshape(a.shape[:-1] + (nc, tf)), a[..., D_FF:].reshape(a.shape[:-1] + (nc, tf))], axis=-1)
    wup = pair(w_ffn_up).transpose(1, 0, 2).astype(_BF16)
    cw = jnp.pad(pair(conv_w).transpose(1, 0, 2), ((0, 0), (0, 5), (0, 0)))
    cb = pair(conv_b.reshape(1, -1)).transpose(1, 0, 2)
    wdn = w_ffn_down.reshape(nc, tf, D_MODEL).astype(_BF16)
    return _out_ffn(attn, rw, x2, w_out.astype(_BF16), row(norm_mix_post), row(norm_ffn_pre), row(norm_ffn_post),
                    wup, cw, cb, wdn, batch, seq)


def kernel(x, norm_mix_pre, norm_mix_post, norm_ffn_pre, norm_ffn_post, w_in, rel_bias, sinks, rwkv_shift_mix, w0, w_decay_up, a0, w_iclr_up, w_gate_up, k_k, k_a, r_k, ln_x_g, ln_x_b, w_out, w_ffn_up, conv_w, conv_b, w_ffn_down):
    batch, seq, _ = x.shape
    depth = w_in.shape[0]
    rel = (jnp.arange(BLOCK)[:, None] + BLOCK) - jnp.arange(2 * BLOCK)[None, :]
    bias = _bias_table(rel_bias.astype(_F32), _t5_bucket(rel).astype(jnp.int32))
    bias = bias.reshape(N_Q_HEADS // 2, 2 * BLOCK, 2 * BLOCK)
    x2 = x.reshape(batch * seq, D_MODEL)
    for l in range(depth):
        x2 = _layer(x2, batch, seq, norm_mix_pre[l], norm_mix_post[l], norm_ffn_pre[l], norm_ffn_post[l], w_in[l],
                    bias, sinks[l], rwkv_shift_mix[l], w0[l], w_decay_up[l], a0[l], w_iclr_up[l], w_gate_up[l],
                    k_k[l], k_a[l], r_k[l].reshape(-1), ln_x_g[l], ln_x_b[l], w_out[l], w_ffn_up[l], conv_w[l],
                    conv_b[l], w_ffn_down[l])
    return x2.reshape(batch, seq, D_MODEL)
```

```python
import math

import jax
import jax.numpy as jnp
from jax import lax
from jax.experimental import pallas as pl
from jax.experimental.pallas import tpu as pltpu

D_MODEL = 1024
HEAD_DIM = 64
D_ATTN = 512
D_RWKV = 512
N_Q_HEADS = 8
N_KV_HEADS = 2
D_KV = 128
WINDOW = 128
BLOCK = 128
N_BUCKETS = 32
MAX_DISTANCE = 128
LORA_DECAY = 64
LORA_ICLR = 64
LORA_GATE = 128
RWKV_COLS = 3 * D_RWKV + LORA_DECAY + LORA_ICLR + LORA_GATE
D_FF = 4 * D_MODEL
NORM_EPS = 1e-6
GN_EPS = 64e-5
NEG_INF = -1e30

IN_PROJ_ROWS = 512
RWKV_ROWS = 512
CHUNK = 64
GROUP_LANES = 256
FFN_ROWS = 512
FFN_COLS = 512
VMEM_LIMIT = 56 * 1024 * 1024

_BF16 = jnp.bfloat16
_F32 = jnp.float32


def _dot(a, b):
    return jnp.dot(a.astype(_BF16), b.astype(_BF16), preferred_element_type=_F32)


def _dot_nt(a, b):
    return lax.dot_general(a.astype(_BF16), b.astype(_BF16), (((1,), (1,)), ((), ())),
                           preferred_element_type=_F32)


def _rms_norm(x, g):
    return x * lax.rsqrt(jnp.mean(x * x, axis=-1, keepdims=True) + NORM_EPS) * g


def _sigmoid(x):
    return 1.0 / (1.0 + jnp.exp(-x))


def _bias_kernel(rb_ref, bucket_ref, out_ref):
    h = pl.program_id(0)
    bucket = bucket_ref[...]
    qi = lax.broadcasted_iota(jnp.int32, (BLOCK, 2 * BLOCK), 0)
    kj = lax.broadcasted_iota(jnp.int32, (BLOCK, 2 * BLOCK), 1)
    rel = qi + BLOCK - kj
    acc = jnp.zeros((BLOCK, 2 * BLOCK), _F32)
    for b in range(N_BUCKETS):
        acc = jnp.where(bucket == b, rb_ref[b, h], acc)
    out_ref[0] = jnp.where((rel >= 0) & (rel < WINDOW), acc, NEG_INF)


def _bias_table(rel_bias, bucket):
    return pl.pallas_call(
        _bias_kernel,
        grid=(N_Q_HEADS,),
        in_specs=[
            pl.BlockSpec(memory_space=pltpu.SMEM),
            pl.BlockSpec((BLOCK, 2 * BLOCK), lambda h: (0, 0)),
        ],
        out_specs=pl.BlockSpec((1, BLOCK, 2 * BLOCK), lambda h: (h, 0, 0)),
        out_shape=jax.ShapeDtypeStruct((N_Q_HEADS, BLOCK, 2 * BLOCK), _F32),
        name="bias_table",
    )(rel_bias, bucket)


_QW = D_ATTN
_KW = 2 * D_KV
_IN_COLS = D_ATTN + 2 * D_KV + RWKV_COLS


def _dup_heads(t):
    low = lax.broadcasted_iota(jnp.int32, (1, D_KV), 1) < HEAD_DIM
    swapped = pltpu.roll(t, HEAD_DIM, 1)
    return jnp.concatenate([jnp.where(low, t, swapped), jnp.where(low, swapped, t)], axis=1)


def _in_proj_kernel(x_ref, g_ref, w_ref, q_ref, k_ref, v_ref, p_ref):
    h = _rms_norm(x_ref[...], g_ref[...]).astype(_BF16)
    q_ref[...] = (_dot(h, w_ref[:, 0:D_ATTN]) * (HEAD_DIM ** -0.5)).astype(_BF16)
    k_ref[...] = _dup_heads(_dot(h, w_ref[:, D_ATTN:D_ATTN + D_KV])).astype(_BF16)
    v_ref[...] = _dup_heads(_dot(h, w_ref[:, D_ATTN + D_KV:D_ATTN + 2 * D_KV])).astype(_BF16)
    p_ref[...] = _dot(h, w_ref[:, D_ATTN + 2 * D_KV:_IN_COLS])


def _in_proj(x2, g, w):
    t = x2.shape[0]
    tm = IN_PROJ_ROWS
    return pl.pallas_call(
        _in_proj_kernel,
        grid=(t // tm,),
        in_specs=[
            pl.BlockSpec((tm, D_MODEL), lambda i: (i, 0)),
            pl.BlockSpec((1, D_MODEL), lambda i: (0, 0)),
            pl.BlockSpec((D_MODEL, _IN_COLS), lambda i: (0, 0)),
        ],
        out_specs=[
            pl.BlockSpec((tm, _QW), lambda i: (i, 0)),
            pl.BlockSpec((tm, _KW), lambda i: (i, 0)),
            pl.BlockSpec((tm, _KW), lambda i: (i, 0)),
            pl.BlockSpec((tm, RWKV_COLS), lambda i: (i, 0)),
        ],
        out_shape=[
            jax.ShapeDtypeStruct((t, _QW), _BF16),
            jax.ShapeDtypeStruct((t, _KW), _BF16),
            jax.ShapeDtypeStruct((t, _KW), _BF16),
            jax.ShapeDtypeStruct((t, RWKV_COLS), _F32),
        ],
        compiler_params=pltpu.CompilerParams(
            dimension_semantics=("arbitrary",), vmem_limit_bytes=VMEM_LIMIT),
        name="in_proj",
    )(x2, g, w)


def _attn_kernel(sink_ref, q_ref, kc_ref, kp_ref, vc_ref, vp_ref, bias_ref, o_ref):
    n = pl.program_id(1)
    col = lax.broadcasted_iota(jnp.int32, (1, 2 * BLOCK), 1)
    valid = col >= jnp.where(n == 0, BLOCK, 0)
    lane = lax.broadcasted_iota(jnp.int32, (1, 2 * HEAD_DIM), 1)
    low = lane < HEAD_DIM
    for i in range(N_Q_HEADS // 2):
        j = (2 * i) // (N_Q_HEADS // N_KV_HEADS)
        lanes = slice(2 * HEAD_DIM * i, 2 * HEAD_DIM * (i + 1))
        kv_lanes = slice(2 * HEAD_DIM * j, 2 * HEAD_DIM * (j + 1))
        qp = q_ref[:, lanes]
        zero = jnp.zeros_like(qp)
        q_st = jnp.concatenate([jnp.where(low, qp, zero), jnp.where(low, zero, qp)], axis=0)
        kd = jnp.concatenate([kp_ref[:, kv_lanes], kc_ref[:, kv_lanes]], axis=0)
        vd = jnp.concatenate([vp_ref[:, kv_lanes], vc_ref[:, kv_lanes]], axis=0)
        s = _dot_nt(q_st, kd) + bias_ref[i]
        s = jnp.where(valid, s, NEG_INF)
        halves = []
        for hh in range(2):
            sh = s[BLOCK * hh:BLOCK * (hh + 1)]
            sink = sink_ref[2 * i + hh]
            m = jnp.maximum(jnp.max(sh, axis=-1, keepdims=True), sink)
            e = jnp.exp(sh - m)
            den = jnp.sum(e, axis=-1, keepdims=True) + jnp.exp(sink - m)
            halves.append(_dot(e, vd) / den)
        o_ref[:, lanes] = jnp.where(low, halves[0], halves[1]).astype(o_ref.dtype)


def _attention(sinks, q, kd, vd, bias, batch, seq):
    nb = seq // BLOCK
    cur = lambda b, n: (b * nb + n, 0)
    prev = lambda b, n: (jnp.maximum(b * nb + n - 1, 0), 0)
    return pl.pallas_call(
        _attn_kernel,
        grid=(batch, nb),
        in_specs=[
            pl.BlockSpec(memory_space=pltpu.SMEM),
            pl.BlockSpec((BLOCK, _QW), cur),
            pl.BlockSpec((BLOCK, _KW), cur),
            pl.BlockSpec((BLOCK, _KW), prev),
            pl.BlockSpec((BLOCK, _KW), cur),
            pl.BlockSpec((BLOCK, _KW), prev),
            pl.BlockSpec((N_Q_HEADS // 2, 2 * BLOCK, 2 * BLOCK), lambda b, n: (0, 0, 0)),
        ],
        out_specs=pl.BlockSpec((BLOCK, D_ATTN), cur),
        out_shape=jax.ShapeDtypeStruct((batch * seq, D_ATTN), _BF16),
        compiler_params=pltpu.CompilerParams(
            dimension_semantics=("arbitrary", "arbitrary"), vmem_limit_bytes=VMEM_LIMIT),
        name="attention",
    )(sinks, q, kd, kd, vd, vd, bias)


_HEADS_PER_GROUP = GROUP_LANES // HEAD_DIM
_N_GROUPS = D_RWKV // GROUP_LANES
_CHUNKS_PER_STEP = 4


def _split3(x):
    hi = x.astype(_BF16)
    r1 = x - hi.astype(_F32)
    mid = r1.astype(_BF16)
    lo = (r1 - mid.astype(_F32)).astype(_BF16)
    return hi, mid, lo


def _rwkv_kernel(p_ref, mix_ref, wl_ref, wg_ref, ones_ref, tri_ref, w0_ref, a0_ref, kk_ref, ka_ref,
                 rk_ref, lng_ref, lnb_ref, o_ref,
                 st_ref, carry_ref, q_s, al_s, kh_s, bh_s, kb_s, bb_s, v_s, plast_s,
                 qh_s, oi_s, mhi_s, mlo_s, n_s, g_s, bo_s, o_s):
    tt = p_ref.shape[0]
    n_chunks = tt // CHUNK

    @pl.when(pl.program_id(1) == 0)
    def _():
        st_ref[...] = jnp.zeros_like(st_ref)
        carry_ref[...] = jnp.zeros_like(carry_ref)

    p = p_ref[...]
    rolled = pltpu.roll(p, 1, 0)
    row8 = lax.broadcasted_iota(jnp.int32, (8, 1), 0)
    head = jnp.where(row8 == 0, carry_ref[...], rolled[0:8])
    carry_ref[...] = rolled[0:8]
    p_prev = jnp.concatenate([head, rolled[8:]], axis=0)
    ps = p + (p_prev - p) * mix_ref[...]

    r = ps[:, 0:D_RWKV]
    k = ps[:, D_RWKV:2 * D_RWKV]
    v = ps[:, 2 * D_RWKV:3 * D_RWKV]
    z = ps[:, 3 * D_RWKV:3 * D_RWKV + LORA_DECAY + LORA_ICLR]
    zg = ps[:, 3 * D_RWKV + LORA_DECAY + LORA_ICLR:RWKV_COLS]

    lane128 = lax.broadcasted_iota(jnp.int32, (1, LORA_DECAY + LORA_ICLR), 1)
    zin = jnp.where(lane128 < LORA_DECAY, jnp.tanh(z), z)
    dl = _dot(zin, wl_ref[...])
    wpre = w0_ref[...] + dl[:, 0:D_RWKV]
    lw = (-math.exp(-0.5)) * _sigmoid(wpre)
    a = _sigmoid(a0_ref[...] + dl[:, D_RWKV:2 * D_RWKV])
    g_s[...] = _dot(_sigmoid(zg), wg_ref[...])

    ones_bd = ones_ref[...]
    kk = k * kk_ref[...]
    kk = kk * lax.rsqrt(jnp.maximum(_dot(kk * kk, ones_bd), 1e-24))
    kp = k * (1.0 + (a - 1.0) * ka_ref[...])
    be = kk * a
    bo_s[...] = _dot(r * kp * rk_ref[...], ones_bd) * v
    v_s[...] = v.astype(_BF16)

    tri = tri_ref[...]
    cums, lasts = [], []
    for c in range(n_chunks):
        hi, mid, lo = _split3(lw[c * CHUNK:(c + 1) * CHUNK])
        cum_c = _dot(tri, hi) + _dot(tri, mid) + _dot(tri, lo)
        last = cum_c[CHUNK - 1:CHUNK, :]
        plast_s[c:c + 1, :] = jnp.exp(last)
        cums.append(cum_c)
        lasts.append(jnp.broadcast_to(last, (CHUNK, D_RWKV)))
    cum = jnp.concatenate(cums, axis=0)
    cum_last = jnp.concatenate(lasts, axis=0)
    q_s[...] = (r * jnp.exp(cum)).astype(_BF16)
    al_s[...] = (-kk * jnp.exp(cum - lw)).astype(_BF16)
    inv = jnp.exp(-cum)
    kh_s[...] = (kp * inv).astype(_BF16)
    bh_s[...] = (be * inv).astype(_BF16)
    rest = jnp.exp(cum_last - cum)
    kb_s[...] = (kp * rest).astype(_BF16)
    bb_s[...] = (be * rest).astype(_BF16)

    gl = GROUP_LANES
    lane = lax.broadcasted_iota(jnp.int32, (1, gl), 1)
    head_masks = [(lane >= HEAD_DIM * h) & (lane < HEAD_DIM * (h + 1)) for h in range(_HEADS_PER_GROUP)]
    s_local = lane & (HEAD_DIM - 1)
    trow = lax.broadcasted_iota(jnp.int32, (CHUNK, 1), 0)
    strict = s_local < trow
    incl = s_local <= trow
    eye_tok = jnp.where(s_local == trow, 1.0, 0.0).astype(_F32)
    grow = lax.broadcasted_iota(jnp.int32, (gl, 1), 0)
    same_head = (grow >> 6) == (lane >> 6)
    diag = grow == lane

    def stack(x):
        xb = x.astype(_BF16)
        zero = jnp.zeros_like(xb)
        return jnp.concatenate([jnp.where(m, xb, zero) for m in head_masks], axis=0)

    def intra_body(it, carry):
        chains = [(it * _CHUNKS_PER_STEP + u, gi) for u in range(_CHUNKS_PER_STEP) for gi in range(_N_GROUPS)]
        idx = [(pl.ds(pl.multiple_of(c * CHUNK, CHUNK), CHUNK), slice(gl * gi, gl * (gi + 1))) for c, gi in chains]
        al = [al_s[r, s] for r, s in idx]
        q = [q_s[r, s] for r, s in idx]
        v = [v_s[r, s] for r, s in idx]
        gram = [_dot_nt(jnp.concatenate([a, b], axis=0),
                        jnp.concatenate([stack(kh_s[r, s]), stack(bh_s[r, s])], axis=0))
                for a, b, (r, s) in zip(al, q, idx)]
        a_ak = [jnp.where(strict, g[0:CHUNK, 0:gl], 0.0) for g in gram]
        a_ab = [jnp.where(strict, g[0:CHUNK, gl:2 * gl], 0.0) for g in gram]
        a_qk = [jnp.where(incl, g[CHUNK:2 * CHUNK, 0:gl], 0.0) for g in gram]
        a_qb = [jnp.where(incl, g[CHUNK:2 * CHUNK, gl:2 * gl], 0.0) for g in gram]
        akv = [_dot(a, stack(b)) for a, b in zip(a_ak, v)]
        pw = [_dot(n, stack(n)) for n in a_ab]
        tinv = [eye_tok + n for n in a_ab]
        for level in range(1, 6):
            rhs = [stack(x) for x in pw]
            if level < 5:
                res = [_dot(jnp.concatenate([x, t], axis=0), w) for x, t, w in zip(pw, tinv, rhs)]
                pw = [x[0:CHUNK] for x in res]
                tinv = [t + x[CHUNK:2 * CHUNK] for t, x in zip(tinv, res)]
            else:
                tinv = [t + _dot(t, w) for t, w in zip(tinv, rhs)]
        y = [_dot(t, jnp.concatenate([stack(a), stack(b)], axis=1)) for t, a, b in zip(tinv, al, akv)]
        al_hat = [x[:, 0:gl] for x in y]
        u_v = [x[:, gl:2 * gl] for x in y]
        qh = [_dot(a, stack(b)) for a, b in zip(a_qb, al_hat)]
        oi = [_dot(jnp.concatenate([a, b], axis=1), jnp.concatenate([stack(c_), stack(d)], axis=0))
              for a, b, c_, d in zip(a_qk, a_qb, v, u_v)]
        upd = [_dot(jnp.concatenate([kb_s[r, s], bb_s[r, s]], axis=0).astype(_F32).T,
                    jnp.concatenate([jnp.concatenate([c_, jnp.zeros_like(c_)], axis=1),
                                     jnp.concatenate([d, e], axis=1).astype(_BF16)], axis=0))
               for (r, s), c_, d, e in zip(idx, v, u_v, al_hat)]
        for (c, gi), (r, s), q_i, qh_i, oi_i, upd_i in zip(chains, idx, q, qh, oi, upd):
            qh_s[r, s] = (q_i.astype(_F32) + qh_i).astype(_BF16)
            oi_s[r, s] = oi_i
            n_s[c, gi] = jnp.where(same_head, upd_i[:, 0:gl], 0.0)
            p_last = plast_s[pl.ds(c, 1), s]
            m_mat = jnp.where(diag, p_last, 0.0) + jnp.where(same_head, upd_i[:, gl:2 * gl], 0.0)
            m_hi = m_mat.astype(_BF16)
            mhi_s[c, gi] = m_hi
            mlo_s[c, gi] = (m_mat - m_hi.astype(_F32)).astype(_BF16)
        return carry

    lax.fori_loop(0, n_chunks // _CHUNKS_PER_STEP, intra_body, 0)

    def state_body(c, carry):
        rows = pl.ds(pl.multiple_of(c * CHUNK, CHUNK), CHUNK)
        sls = [slice(gl * gi, gl * (gi + 1)) for gi in range(_N_GROUPS)]
        st = [st_ref[gi] for gi in range(_N_GROUPS)]
        s_hi = [x.astype(_BF16) for x in st]
        s_lo = [(x - h.astype(_F32)).astype(_BF16) for x, h in zip(st, s_hi)]
        new = [_dot(jnp.concatenate([mhi_s[c, gi], mhi_s[c, gi], mlo_s[c, gi]], axis=1),
                    jnp.concatenate([s_hi[gi], s_lo[gi], s_hi[gi]], axis=0)) for gi in range(_N_GROUPS)]
        out = [_dot(qh_s[rows, sls[gi]], s_hi[gi]) for gi in range(_N_GROUPS)]
        for gi in range(_N_GROUPS):
            st_ref[gi] = new[gi] + n_s[c, gi]
            o_s[rows, sls[gi]] = out[gi] + oi_s[rows, sls[gi]]
        return carry

    lax.fori_loop(0, n_chunks, state_body, 0)

    o = o_s[...]
    mu = _dot(o, ones_bd) * (1.0 / HEAD_DIM)
    d = o - mu
    var = _dot(d * d, ones_bd) * (1.0 / HEAD_DIM)
    on = d * lax.rsqrt(var + GN_EPS) * lng_ref[...] + lnb_ref[...]
    o_ref[...] = ((on + bo_s[...]) * g_s[...]).astype(o_ref.dtype)


def _rwkv(p, mix, wl, wg, ones_bd, tri, w0, a0, k_k, k_a, r_k, ln_g, ln_b, batch, seq):
    tt = RWKV_ROWS
    nt = seq // tt
    vec = lambda n: pl.BlockSpec((1, n), lambda b, t: (0, 0))
    full = lambda a: pl.BlockSpec(a.shape, lambda b, t: (0,) * a.ndim)
    tile = lambda dt: pltpu.VMEM((tt, D_RWKV), dt)
    per_chunk = lambda dt: pltpu.VMEM((tt // CHUNK, _N_GROUPS, GROUP_LANES, GROUP_LANES), dt)
    return pl.pallas_call(
        _rwkv_kernel,
        grid=(batch, nt),
        in_specs=[
            pl.BlockSpec((tt, RWKV_COLS), lambda b, t: (b * nt + t, 0)),
            vec(RWKV_COLS), full(wl), full(wg), full(ones_bd), full(tri),
            vec(D_RWKV), vec(D_RWKV), vec(D_RWKV), vec(D_RWKV), vec(D_RWKV), vec(D_RWKV), vec(D_RWKV),
        ],
        out_specs=pl.BlockSpec((tt, D_RWKV), lambda b, t: (b * nt + t, 0)),
        out_shape=jax.ShapeDtypeStruct((batch * seq, D_RWKV), _BF16),
        scratch_shapes=[
            pltpu.VMEM((_N_GROUPS, GROUP_LANES, GROUP_LANES), _F32),
            pltpu.VMEM((8, RWKV_COLS), _F32),
            tile(_BF16), tile(_BF16), tile(_BF16), tile(_BF16), tile(_BF16), tile(_BF16), tile(_BF16),
            pltpu.VMEM((tt // CHUNK, D_RWKV), _F32),
            tile(_BF16), tile(_F32), per_chunk(_BF16), per_chunk(_BF16), per_chunk(_F32),
            tile(_F32), tile(_F32), tile(_F32),
        ],
        compiler_params=pltpu.CompilerParams(
            dimension_semantics=("arbitrary", "arbitrary"), vmem_limit_bytes=VMEM_LIMIT),
        name="rwkv",
    )(p, mix, wl, wg, ones_bd, tri, w0, a0, k_k, k_a, r_k, ln_g, ln_b)


_N_FF_STEPS = D_FF // FFN_COLS
_PIECE_ROWS = 64
_PIECE_LANES = 128


_GELU_C1 = 0.7978845608028654
_GELU_C2 = _GELU_C1 * 0.044715


def _two_gelu_tanh(x):
    return x + x * jnp.tanh(x * (_GELU_C1 + _GELU_C2 * (x * x)))


def _out_ffn_kernel(attn_ref, rw_ref, x_ref, wo_ref, gmp_ref, gfp_ref, gfo_ref,
                    wup_ref, cw_ref, cb_ref, wdn_ref, out_ref,
                    carry_ref, h_ref, x1_ref, u_s, act_s):
    tm = x_ref.shape[0]

    @pl.when(pl.program_id(1) == 0)
    def _():
        carry_ref[...] = jnp.zeros_like(carry_ref)

    mix = _dot(attn_ref[...], wo_ref[0:D_ATTN, :]) + _dot(rw_ref[...], wo_ref[D_ATTN:D_MODEL, :])
    x1 = x_ref[...] + _rms_norm(mix, gmp_ref[...])
    x1_ref[...] = x1
    h_ref[...] = _rms_norm(x1, gfp_ref[...]).astype(_BF16)

    def up(c):
        slot = c % 2
        h = h_ref[...]
        for half, col0 in enumerate((c * FFN_COLS, D_FF + c * FFN_COLS)):
            u = jnp.dot(h, wup_ref[:, col0:col0 + FFN_COLS], preferred_element_type=_F32)
            lanes = slice(half * FFN_COLS, (half + 1) * FFN_COLS)
            u_s[slot, 8:tm + 8, lanes] = u
            u_s[slot, 0:8, lanes] = carry_ref[:, col0:col0 + FFN_COLS]
            carry_ref[:, col0:col0 + FFN_COLS] = u[tm - 8:tm]

    def conv_act(c):
        slot = c % 2

        def conv(r0, col, wcol):
            w = u_s[slot, r0:r0 + _PIECE_ROWS + 8, col:col + _PIECE_LANES]
            lanes = slice(wcol, wcol + _PIECE_LANES)
            r1 = pltpu.roll(w, 1, 0)[8:]
            r2 = pltpu.roll(w, 2, 0)[8:]
            return cb_ref[:, lanes] + cw_ref[0:1, lanes] * r2 + cw_ref[1:2, lanes] * r1 + cw_ref[2:3, lanes] * w[8:]

        for r0 in range(0, tm, _PIECE_ROWS):
            for col in range(0, FFN_COLS, _PIECE_LANES):
                gate = conv(r0, col, c * FFN_COLS + col)
                half_val = conv(r0, FFN_COLS + col, D_FF + c * FFN_COLS + col)
                act_s[slot, r0:r0 + _PIECE_ROWS, col:col + _PIECE_LANES] = (
                    _two_gelu_tanh(gate) * half_val).astype(_BF16)

    acc = None
    up(0)
    for c in range(_N_FF_STEPS):
        if c + 1 < _N_FF_STEPS:
            up(c + 1)
        conv_act(c)
        d = jnp.dot(act_s[c % 2], wdn_ref[c], preferred_element_type=_F32)
        if c == 0:
            out_ref[...] = d
        else:
            out_ref[...] += d
    out_ref[...] = x1_ref[...] + _rms_norm(out_ref[...], gfo_ref[...])


def _out_ffn(attn, rw, x2, wo, g_mix_post, g_ffn_pre, g_ffn_post, wup, cw, cb, wdn, batch, seq):
    tm = FFN_ROWS
    nt = seq // tm
    rows = lambda n: pl.BlockSpec((tm, n), lambda b, t: (b * nt + t, 0))
    vec = pl.BlockSpec((1, D_MODEL), lambda b, t: (0, 0))
    const = lambda a: pl.BlockSpec(a.shape, lambda b, t: (0,) * a.ndim, pipeline_mode=pl.Buffered(1))
    return pl.pallas_call(
        _out_ffn_kernel,
        grid=(batch, nt),
        in_specs=[rows(D_ATTN), rows(D_RWKV), rows(D_MODEL), const(wo), vec, vec, vec,
                  const(wup), const(cw), const(cb), const(wdn)],
        out_specs=rows(D_MODEL),
        out_shape=jax.ShapeDtypeStruct((batch * seq, D_MODEL), _F32),
        scratch_shapes=[
            pltpu.VMEM((8, 2 * D_FF), _F32),
            pltpu.VMEM((tm, D_MODEL), _BF16),
            pltpu.VMEM((tm, D_MODEL), _F32),
            pltpu.VMEM((2, tm + 8, 2 * FFN_COLS), _F32),
            pltpu.VMEM((2, tm, FFN_COLS), _BF16),
        ],
        compiler_params=pltpu.CompilerParams(
            dimension_semantics=("arbitrary", "arbitrary"), vmem_limit_bytes=VMEM_LIMIT),
        name="out_ffn",
    )(attn, rw, x2, wo, g_mix_post, g_ffn_pre, g_ffn_post, wup, cw, cb, wdn)


def _t5_bucket(rel):
    n = jnp.maximum(rel, 0)
    max_exact = N_BUCKETS // 2
    large = max_exact + (jnp.log(jnp.maximum(n, 1).astype(_F32) / max_exact)
                         / math.log(MAX_DISTANCE / max_exact)
                         * (N_BUCKETS - max_exact)).astype(jnp.int32)
    large = jnp.minimum(large, N_BUCKETS - 1)
    return jnp.where(n < max_exact, n, large)


def _row(a):
    return a.reshape(1, -1)


def _in_proj_stage(x2, norm_mix_pre, w_in):
    return _in_proj(x2, _row(norm_mix_pre), w_in.astype(_BF16))


def _rwkv_stage(p, shift_mix, w0, w_decay_up, a0, w_iclr_up, w_gate_up, k_k, k_a, r_k, ln_x_g, ln_x_b, batch, seq):
    zeros = jnp.zeros((LORA_DECAY, D_RWKV), _F32)
    w_lora = jnp.concatenate([jnp.concatenate([w_decay_up, zeros], axis=1),
                              jnp.concatenate([zeros, w_iclr_up], axis=1)], axis=0).astype(_BF16)
    hid = jnp.arange(D_RWKV) // HEAD_DIM
    ones_bd = (hid[:, None] == hid[None, :]).astype(_BF16)
    tri = (jnp.arange(CHUNK)[:, None] >= jnp.arange(CHUNK)[None, :]).astype(_BF16)
    return _rwkv(p, _row(shift_mix), w_lora, w_gate_up.astype(_BF16), ones_bd, tri, _row(w0), _row(a0), _row(k_k),
                 _row(k_a), _row(r_k), _row(ln_x_g), _row(ln_x_b), batch, seq)


def _ffn_stage(attn, rw, x2, norm_mix_post, norm_ffn_pre, norm_ffn_post, w_out, w_ffn_up, conv_w, conv_b,
               w_ffn_down, batch, seq):
    half_val = jnp.concatenate([jnp.ones((D_FF,), _F32), jnp.full((D_FF,), 0.5, _F32)])
    cw = jnp.pad(conv_w * half_val, ((0, 8 - conv_w.shape[0]), (0, 0)))
    cb = _row(conv_b * half_val)
    wdn = w_ffn_down.reshape(_N_FF_STEPS, FFN_COLS, D_MODEL).astype(_BF16)
    return _out_ffn(attn, rw, x2, w_out.astype(_BF16), _row(norm_mix_post), _row(norm_ffn_pre),
                    _row(norm_ffn_post), w_ffn_up.astype(_BF16), cw, cb, wdn, batch, seq)


def _layer(x2, batch, seq, norm_mix_pre, norm_mix_post, norm_ffn_pre, norm_ffn_post, w_in, bias, sinks,
           shift_mix, w0, w_decay_up, a0, w_iclr_up, w_gate_up, k_k, k_a, r_k, ln_x_g, ln_x_b,
           w_out, w_ffn_up, conv_w, conv_b, w_ffn_down):
    q, kd, vd, p = _in_proj_stage(x2, norm_mix_pre, w_in)
    attn = _attention(sinks, q, kd, vd, bias, batch, seq)
    rw = _rwkv_stage(p, shift_mix, w0, w_decay_up, a0, w_iclr_up, w_gate_up, k_k, k_a, r_k, ln_x_g, ln_x_b,
                     batch, seq)
    return _ffn_stage(attn, rw, x2, norm_mix_post, norm_ffn_pre, norm_ffn_post, w_out, w_ffn_up, conv_w, conv_b,
                      w_ffn_down, batch, seq)


def kernel(x, norm_mix_pre, norm_mix_post, norm_ffn_pre, norm_ffn_post, w_in, rel_bias, sinks, rwkv_shift_mix, w0, w_decay_up, a0, w_iclr_up, w_gate_up, k_k, k_a, r_k, ln_x_g, ln_x_b, w_out, w_ffn_up, conv_w, conv_b, w_ffn_down):
    batch, seq, _ = x.shape
    depth = w_in.shape[0]
    rel = (jnp.arange(BLOCK)[:, None] + BLOCK) - jnp.arange(2 * BLOCK)[None, :]
    bias = _bias_table(rel_bias.astype(_F32), _t5_bucket(rel).astype(jnp.int32))
    bias = bias.reshape(N_Q_HEADS // 2, 2 * BLOCK, 2 * BLOCK)
    x2 = x.reshape(batch * seq, D_MODEL)
    for l in range(depth):
        x2 = _layer(x2, batch, seq, norm_mix_pre[l], norm_mix_post[l], norm_ffn_pre[l], norm_ffn_post[l], w_in[l],
                    bias, sinks[l], rwkv_shift_mix[l], w0[l], w_decay_up[l], a0[l], w_iclr_up[l], w_gate_up[l],
                    k_k[l], k_a[l], r_k[l].reshape(-1), ln_x_g[l], ln_x_b[l], w_out[l], w_ffn_up[l], conv_w[l],
                    conv_b[l], w_ffn_down[l])
    return x2.reshape(batch, seq, D_MODEL)
```

```python
import functools
import math

import jax
import jax.numpy as jnp
import numpy as np
from jax import lax
from jax.experimental import pallas as pl
from jax.experimental.pallas import tpu as pltpu

D_MODEL = 1024
HEAD_DIM = 64
D_ATTN = 512
D_RWKV = 512
N_Q_HEADS = 8
N_KV_HEADS = 2
D_KV = 128
WINDOW = 128
BLOCK = 128
N_BUCKETS = 32
MAX_DISTANCE = 128
LORA_DECAY = 64
LORA_ICLR = 64
LORA_GATE = 128
RWKV_COLS = 3 * D_RWKV + LORA_DECAY + LORA_ICLR + LORA_GATE
D_FF = 4 * D_MODEL
NORM_EPS = 1e-6
GN_EPS = 64e-5
NEG_INF = -1e30

IN_PROJ_ROWS = 512
RWKV_ROWS = 512
CHUNK = 64
GROUP_LANES = 256
FFN_ROWS = 512
FFN_COLS = 512
VMEM_LIMIT = 56 * 1024 * 1024

_BF16 = jnp.bfloat16
_F32 = jnp.float32


def _dot(a, b):
    return jnp.dot(a.astype(_BF16), b.astype(_BF16), preferred_element_type=_F32)


def _dot_nt(a, b):
    return lax.dot_general(a.astype(_BF16), b.astype(_BF16), (((1,), (1,)), ((), ())),
                           preferred_element_type=_F32)


def _rms_norm(x, g):
    return x * lax.rsqrt(jnp.mean(x * x, axis=-1, keepdims=True) + NORM_EPS) * g


def _sigmoid(x):
    return 0.5 + 0.5 * jnp.tanh(0.5 * x)


def _bias_kernel(rb_ref, bucket_ref, out_ref):
    h = pl.program_id(0)
    bucket = bucket_ref[...]
    qi = lax.broadcasted_iota(jnp.int32, (BLOCK, 2 * BLOCK), 0)
    kj = lax.broadcasted_iota(jnp.int32, (BLOCK, 2 * BLOCK), 1)
    rel = qi + BLOCK - kj
    acc = jnp.zeros((BLOCK, 2 * BLOCK), _F32)
    for b in range(N_BUCKETS):
        acc = jnp.where(bucket == b, rb_ref[b, h], acc)
    out_ref[0] = jnp.where((rel >= 0) & (rel < WINDOW), acc, NEG_INF)


def _bias_table(rel_bias, bucket):
    return pl.pallas_call(
        _bias_kernel,
        grid=(N_Q_HEADS,),
        in_specs=[
            pl.BlockSpec(memory_space=pltpu.SMEM),
            pl.BlockSpec((BLOCK, 2 * BLOCK), lambda h: (0, 0)),
        ],
        out_specs=pl.BlockSpec((1, BLOCK, 2 * BLOCK), lambda h: (h, 0, 0)),
        out_shape=jax.ShapeDtypeStruct((N_Q_HEADS, BLOCK, 2 * BLOCK), _F32),
        name="bias_table",
    )(rel_bias, bucket)


_QW = D_ATTN
_KW = 2 * D_KV
_IN_COLS = D_ATTN + 2 * D_KV + RWKV_COLS


def _dup_heads(t):
    low = lax.broadcasted_iota(jnp.int32, (1, D_KV), 1) < HEAD_DIM
    swapped = pltpu.roll(t, HEAD_DIM, 1)
    return jnp.concatenate([jnp.where(low, t, swapped), jnp.where(low, swapped, t)], axis=1)


def _in_proj_kernel(tiles_per_seq, x_ref, g_ref, w_ref, mix_ref, q_ref, k_ref, v_ref, p_ref, carry_ref):
    @pl.when(pl.program_id(0) % tiles_per_seq == 0)
    def _():
        carry_ref[...] = jnp.zeros_like(carry_ref)

    h = _rms_norm(x_ref[...], g_ref[...]).astype(_BF16)
    q_ref[...] = (_dot(h, w_ref[:, 0:D_ATTN]) * (HEAD_DIM ** -0.5)).astype(_BF16)
    k_ref[...] = _dup_heads(_dot(h, w_ref[:, D_ATTN:D_ATTN + D_KV])).astype(_BF16)
    v_ref[...] = _dup_heads(_dot(h, w_ref[:, D_ATTN + D_KV:D_ATTN + 2 * D_KV])).astype(_BF16)
    p = _dot(h, w_ref[:, D_ATTN + 2 * D_KV:_IN_COLS])
    rolled = pltpu.roll(p, 1, 0)
    row8 = lax.broadcasted_iota(jnp.int32, (8, 1), 0)
    head = jnp.where(row8 == 0, carry_ref[...], rolled[0:8])
    carry_ref[...] = rolled[0:8]
    p_prev = jnp.concatenate([head, rolled[8:]], axis=0)
    p_ref[...] = p + (p_prev - p) * mix_ref[...]


def _in_proj(x2, g, w, mix, seq):
    t = x2.shape[0]
    tm = IN_PROJ_ROWS
    return pl.pallas_call(
        functools.partial(_in_proj_kernel, seq // tm),
        grid=(t // tm,),
        in_specs=[
            pl.BlockSpec((tm, D_MODEL), lambda i: (i, 0)),
            pl.BlockSpec((1, D_MODEL), lambda i: (0, 0)),
            pl.BlockSpec((D_MODEL, _IN_COLS), lambda i: (0, 0)),
            pl.BlockSpec((1, RWKV_COLS), lambda i: (0, 0)),
        ],
        scratch_shapes=[pltpu.VMEM((8, RWKV_COLS), _F32)],
        out_specs=[
            pl.BlockSpec((tm, _QW), lambda i: (i, 0)),
            pl.BlockSpec((tm, _KW), lambda i: (i, 0)),
            pl.BlockSpec((tm, _KW), lambda i: (i, 0)),
            pl.BlockSpec((tm, RWKV_COLS), lambda i: (i, 0)),
        ],
        out_shape=[
            jax.ShapeDtypeStruct((t, _QW), _BF16),
            jax.ShapeDtypeStruct((t, _KW), _BF16),
            jax.ShapeDtypeStruct((t, _KW), _BF16),
            jax.ShapeDtypeStruct((t, RWKV_COLS), _F32),
        ],
        compiler_params=pltpu.CompilerParams(
            dimension_semantics=("arbitrary",), vmem_limit_bytes=VMEM_LIMIT),
        name="in_proj",
    )(x2, g, w, mix)


def _attn_kernel(sink_ref, q_ref, kc_ref, kp_ref, vc_ref, vp_ref, bias_ref, o_ref):
    n = pl.program_id(1)
    col = lax.broadcasted_iota(jnp.int32, (1, 2 * BLOCK), 1)
    valid = col >= jnp.where(n == 0, BLOCK, 0)
    lane = lax.broadcasted_iota(jnp.int32, (1, 2 * HEAD_DIM), 1)
    low = lane < HEAD_DIM
    for i in range(N_Q_HEADS // 2):
        j = (2 * i) // (N_Q_HEADS // N_KV_HEADS)
        lanes = slice(2 * HEAD_DIM * i, 2 * HEAD_DIM * (i + 1))
        kv_lanes = slice(2 * HEAD_DIM * j, 2 * HEAD_DIM * (j + 1))
        qp = q_ref[:, lanes]
        zero = jnp.zeros_like(qp)
        q_st = jnp.concatenate([jnp.where(low, qp, zero), jnp.where(low, zero, qp)], axis=0)
        kd = jnp.concatenate([kp_ref[:, kv_lanes], kc_ref[:, kv_lanes]], axis=0)
        vd = jnp.concatenate([vp_ref[:, kv_lanes], vc_ref[:, kv_lanes]], axis=0)
        s = _dot_nt(q_st, kd) + bias_ref[i]
        s = jnp.where(valid, s, NEG_INF)
        halves = []
        for hh in range(2):
            sh = s[BLOCK * hh:BLOCK * (hh + 1)]
            sink = sink_ref[2 * i + hh]
            m = jnp.maximum(jnp.max(sh, axis=-1, keepdims=True), sink)
            e = jnp.exp(sh - m)
            den = jnp.sum(e, axis=-1, keepdims=True) + jnp.exp(sink - m)
            halves.append(_dot(e, vd) / den)
        o_ref[:, lanes] = jnp.where(low, halves[0], halves[1]).astype(o_ref.dtype)


def _attention(sinks, q, kd, vd, bias, batch, seq):
    nb = seq // BLOCK
    cur = lambda b, n: (b * nb + n, 0)
    prev = lambda b, n: (jnp.maximum(b * nb + n - 1, 0), 0)
    return pl.pallas_call(
        _attn_kernel,
        grid=(batch, nb),
        in_specs=[
            pl.BlockSpec(memory_space=pltpu.SMEM),
            pl.BlockSpec((BLOCK, _QW), cur),
            pl.BlockSpec((BLOCK, _KW), cur),
            pl.BlockSpec((BLOCK, _KW), prev),
            pl.BlockSpec((BLOCK, _KW), cur),
            pl.BlockSpec((BLOCK, _KW), prev),
            pl.BlockSpec((N_Q_HEADS // 2, 2 * BLOCK, 2 * BLOCK), lambda b, n: (0, 0, 0)),
        ],
        out_specs=pl.BlockSpec((BLOCK, D_ATTN), cur),
        out_shape=jax.ShapeDtypeStruct((batch * seq, D_ATTN), _BF16),
        compiler_params=pltpu.CompilerParams(
            dimension_semantics=("arbitrary", "arbitrary"), vmem_limit_bytes=VMEM_LIMIT),
        name="attention",
    )(sinks, q, kd, kd, vd, vd, bias)


_HEADS_PER_GROUP = GROUP_LANES // HEAD_DIM
_N_GROUPS = D_RWKV // GROUP_LANES
_CHUNKS_PER_STEP = 8


def _split3(x):
    hi = x.astype(_BF16)
    r1 = x - hi.astype(_F32)
    mid = r1.astype(_BF16)
    lo = (r1 - mid.astype(_F32)).astype(_BF16)
    return hi, mid, lo


def _rwkv_kernel(p_ref, wl_ref, wg_ref, ones_ref, tri_ref, w0_ref, a0_ref, kk_ref, ka_ref,
                 rk_ref, lng_ref, lnb_ref, o_ref,
                 st_ref, q_s, al_s, kh_s, bh_s, kb_s, bb_s, v_s, plast_s,
                 qh_s, oi_s, lr_s, dcol_s, n_s, g_s, bo_s, o_s):
    tt = p_ref.shape[0]
    n_chunks = tt // CHUNK

    @pl.when(pl.program_id(1) == 0)
    def _():
        st_ref[...] = jnp.zeros_like(st_ref)

    r = p_ref[:, 0:D_RWKV]
    k = p_ref[:, D_RWKV:2 * D_RWKV]
    v = p_ref[:, 2 * D_RWKV:3 * D_RWKV]
    z = p_ref[:, 3 * D_RWKV:3 * D_RWKV + LORA_DECAY + LORA_ICLR]
    zg = p_ref[:, 3 * D_RWKV + LORA_DECAY + LORA_ICLR:RWKV_COLS]

    lane128 = lax.broadcasted_iota(jnp.int32, (1, LORA_DECAY + LORA_ICLR), 1)
    zin = jnp.where(lane128 < LORA_DECAY, jnp.tanh(z), z)
    dl = _dot(zin, wl_ref[...])
    wpre = w0_ref[...] + dl[:, 0:D_RWKV]
    lw = (-math.exp(-0.5)) * _sigmoid(wpre)
    a = _sigmoid(a0_ref[...] + dl[:, D_RWKV:2 * D_RWKV])
    g_s[...] = _dot(_sigmoid(zg), wg_ref[...])

    ones_bd = ones_ref[...]
    kk = k * kk_ref[...]
    kk = kk * lax.rsqrt(jnp.maximum(_dot(kk * kk, ones_bd), 1e-24))
    kp = k * (1.0 + (a - 1.0) * ka_ref[...])
    be = kk * a
    bo_s[...] = _dot(r * kp * rk_ref[...], ones_bd) * v
    v_s[...] = v.astype(_BF16)

    tri = tri_ref[...]
    cums, lasts = [], []
    for c in range(n_chunks):
        hi, mid, lo = _split3(lw[c * CHUNK:(c + 1) * CHUNK])
        cum_c = _dot(tri, hi) + _dot(tri, mid) + _dot(tri, lo)
        p_last = jnp.exp(cum_c[CHUNK - 1:CHUNK, :])
        plast_s[c:c + 1, :] = p_last
        cums.append(cum_c)
        lasts.append(jnp.broadcast_to(p_last, (CHUNK, D_RWKV)))
    cum = jnp.concatenate(cums, axis=0)
    q_s[...] = (r * jnp.exp(cum)).astype(_BF16)
    al_s[...] = (-kk * jnp.exp(cum - lw)).astype(_BF16)
    inv = jnp.exp(-cum)
    kh_s[...] = (kp * inv).astype(_BF16)
    bh_s[...] = (be * inv).astype(_BF16)
    rest = jnp.concatenate(lasts, axis=0) * inv
    kb_s[...] = (kp * rest).astype(_BF16)
    bb_s[...] = (be * rest).astype(_BF16)

    gl = GROUP_LANES
    lane = lax.broadcasted_iota(jnp.int32, (1, gl), 1)
    head_masks = [(lane >= HEAD_DIM * h) & (lane < HEAD_DIM * (h + 1)) for h in range(_HEADS_PER_GROUP)]
    s_local = lane & (HEAD_DIM - 1)
    trow = lax.broadcasted_iota(jnp.int32, (CHUNK, 1), 0)
    strict = s_local < trow
    incl = s_local <= trow
    eye_tok = jnp.where(s_local == trow, 1.0, 0.0).astype(_F32)
    grow = lax.broadcasted_iota(jnp.int32, (gl, 1), 0)
    same_head = (grow >> 6) == (lane >> 6)

    def stack(x):
        xb = x.astype(_BF16)
        zero = jnp.zeros_like(xb)
        return jnp.concatenate([jnp.where(m, xb, zero) for m in head_masks], axis=0)

    def intra_body(it, carry):
        chains = [(it * _CHUNKS_PER_STEP + u, gi) for u in range(_CHUNKS_PER_STEP) for gi in range(_N_GROUPS)]
        idx = [(pl.ds(pl.multiple_of(c * CHUNK, CHUNK), CHUNK), slice(gl * gi, gl * (gi + 1))) for c, gi in chains]
        al = [al_s[r, s] for r, s in idx]
        q = [q_s[r, s] for r, s in idx]
        v = [v_s[r, s] for r, s in idx]
        gram = [_dot_nt(jnp.concatenate([a, b], axis=0),
                        jnp.concatenate([stack(kh_s[r, s]), stack(bh_s[r, s])], axis=0))
                for a, b, (r, s) in zip(al, q, idx)]
        a_ak = [jnp.where(strict, g[0:CHUNK, 0:gl], 0.0) for g in gram]
        a_ab = [jnp.where(strict, g[0:CHUNK, gl:2 * gl], 0.0) for g in gram]
        a_qk = [jnp.where(incl, g[CHUNK:2 * CHUNK, 0:gl], 0.0) for g in gram]
        a_qb = [jnp.where(incl, g[CHUNK:2 * CHUNK, gl:2 * gl], 0.0) for g in gram]
        akv = [_dot(a, stack(b)) for a, b in zip(a_ak, v)]
        pw = [_dot(n, stack(n)) for n in a_ab]
        tinv = [eye_tok + n for n in a_ab]
        for level in range(1, 6):
            rhs = [stack(x) for x in pw]
            if level < 5:
                res = [_dot(jnp.concatenate([x, t], axis=0), w) for x, t, w in zip(pw, tinv, rhs)]
                pw = [x[0:CHUNK] for x in res]
                tinv = [t + x[CHUNK:2 * CHUNK] for t, x in zip(tinv, res)]
            else:
                tinv = [t + _dot(t, w) for t, w in zip(tinv, rhs)]
        y = [_dot(t, jnp.concatenate([stack(a), stack(b)], axis=1)) for t, a, b in zip(tinv, al, akv)]
        al_hat = [x[:, 0:gl] for x in y]
        u_v = [x[:, gl:2 * gl] for x in y]
        qh = [_dot(a, stack(b)) for a, b in zip(a_qb, al_hat)]
        oi = [_dot(jnp.concatenate([a, b], axis=1), jnp.concatenate([stack(c_), stack(d)], axis=0))
              for a, b, c_, d in zip(a_qk, a_qb, v, u_v)]
        tr = [jnp.concatenate([kb_s[r, s].astype(_F32), bb_s[r, s].astype(_F32),
                               jnp.broadcast_to(plast_s[pl.ds(c, 1), s], (2 * CHUNK, gl))], axis=0).T
              for (c, gi), (r, s) in zip(chains, idx)]
        upd = [_dot(t[:, 0:2 * CHUNK],
                    jnp.concatenate([jnp.concatenate([c_, jnp.zeros_like(c_)], axis=1),
                                     jnp.concatenate([d, e], axis=1).astype(_BF16)], axis=0))
               for t, c_, d, e in zip(tr, v, u_v, al_hat)]
        for (c, gi), (r, s), q_i, qh_i, oi_i, upd_i, t in zip(chains, idx, q, qh, oi, upd, tr):
            qh_s[r, s] = (q_i.astype(_F32) + qh_i).astype(_BF16)
            oi_s[r, s] = oi_i
            n_s[c, gi] = jnp.where(same_head, upd_i[:, 0:gl], 0.0)
            lr_s[c, gi] = jnp.where(same_head, upd_i[:, gl:2 * gl], 0.0).astype(_BF16)
            dcol_s[c, gi] = t[:, 2 * CHUNK:4 * CHUNK]
        return carry

    lax.fori_loop(0, n_chunks // _CHUNKS_PER_STEP, intra_body, 0)

    def state_body(c, carry):
        rows = pl.ds(pl.multiple_of(c * CHUNK, CHUNK), CHUNK)
        sls = [slice(gl * gi, gl * (gi + 1)) for gi in range(_N_GROUPS)]
        st = [st_ref[gi] for gi in range(_N_GROUPS)]
        sb = [x.astype(_BF16) for x in st]
        new = [_dot(lr_s[c, gi], sb[gi]) for gi in range(_N_GROUPS)]
        out = [_dot(qh_s[rows, sls[gi]], sb[gi]) for gi in range(_N_GROUPS)]
        for gi in range(_N_GROUPS):
            dcol = dcol_s[c, gi]
            st_ref[gi] = jnp.concatenate([dcol] * (gl // (2 * CHUNK)), axis=1) * st[gi] + new[gi] + n_s[c, gi]
            o_s[rows, sls[gi]] = out[gi] + oi_s[rows, sls[gi]]
        return carry

    lax.fori_loop(0, n_chunks, state_body, 0)

    o = o_s[...]
    mu = _dot(o, ones_bd) * (1.0 / HEAD_DIM)
    d = o - mu
    var = _dot(d * d, ones_bd) * (1.0 / HEAD_DIM)
    on = d * lax.rsqrt(var + GN_EPS) * lng_ref[...] + lnb_ref[...]
    o_ref[...] = ((on + bo_s[...]) * g_s[...]).astype(o_ref.dtype)


def _rwkv(p, wl, wg, ones_bd, tri, w0, a0, k_k, k_a, r_k, ln_g, ln_b, batch, seq):
    tt = RWKV_ROWS
    nt = seq // tt
    vec = lambda n: pl.BlockSpec((1, n), lambda b, t: (0, 0))
    full = lambda a: pl.BlockSpec(a.shape, lambda b, t: (0,) * a.ndim)
    tile = lambda dt: pltpu.VMEM((tt, D_RWKV), dt)
    per_chunk = lambda dt: pltpu.VMEM((tt // CHUNK, _N_GROUPS, GROUP_LANES, GROUP_LANES), dt)
    return pl.pallas_call(
        _rwkv_kernel,
        grid=(batch, nt),
        in_specs=[
            pl.BlockSpec((tt, RWKV_COLS), lambda b, t: (b * nt + t, 0)),
            full(wl), full(wg), full(ones_bd), full(tri),
            vec(D_RWKV), vec(D_RWKV), vec(D_RWKV), vec(D_RWKV), vec(D_RWKV), vec(D_RWKV), vec(D_RWKV),
        ],
        out_specs=pl.BlockSpec((tt, D_RWKV), lambda b, t: (b * nt + t, 0)),
        out_shape=jax.ShapeDtypeStruct((batch * seq, D_RWKV), _BF16),
        scratch_shapes=[
            pltpu.VMEM((_N_GROUPS, GROUP_LANES, GROUP_LANES), _F32),
            tile(_BF16), tile(_BF16), tile(_BF16), tile(_BF16), tile(_BF16), tile(_BF16), tile(_BF16),
            pltpu.VMEM((tt // CHUNK, D_RWKV), _F32),
            tile(_BF16), tile(_F32), per_chunk(_BF16),
            pltpu.VMEM((tt // CHUNK, _N_GROUPS, GROUP_LANES, 2 * CHUNK), _F32), per_chunk(_F32),
            tile(_F32), tile(_F32), tile(_F32),
        ],
        compiler_params=pltpu.CompilerParams(
            dimension_semantics=("arbitrary", "arbitrary"), vmem_limit_bytes=VMEM_LIMIT),
        name="rwkv",
    )(p, wl, wg, ones_bd, tri, w0, a0, k_k, k_a, r_k, ln_g, ln_b)


_N_FF_STEPS = D_FF // FFN_COLS
_PIECE_ROWS = 64
_PIECE_LANES = 128


_GELU_C1 = 0.7978845608028654
_GELU_C2 = _GELU_C1 * 0.044715


def _two_gelu_tanh(x):
    return x + x * jnp.tanh(x * (_GELU_C1 + _GELU_C2 * (x * x)))


def _out_ffn_kernel(attn_ref, rw_ref, x_ref, wo_ref, gmp_ref, gfp_ref, gfo_ref,
                    wup_ref, cw_ref, cb_ref, wdn_ref, out_ref,
                    carry_ref, h_ref, x1_ref, u_s, act_s):
    tm = x_ref.shape[0]

    @pl.when(pl.program_id(1) == 0)
    def _():
        carry_ref[...] = jnp.zeros_like(carry_ref)

    mix = _dot(attn_ref[...], wo_ref[0:D_ATTN, :]) + _dot(rw_ref[...], wo_ref[D_ATTN:D_MODEL, :])
    x1 = x_ref[...] + _rms_norm(mix, gmp_ref[...])
    x1_ref[...] = x1
    h_ref[...] = _rms_norm(x1, gfp_ref[...]).astype(_BF16)

    def up(c):
        slot = c % 2
        h = h_ref[...]
        for half, col0 in enumerate((c * FFN_COLS, D_FF + c * FFN_COLS)):
            u = jnp.dot(h, wup_ref[:, col0:col0 + FFN_COLS], preferred_element_type=_F32)
            lanes = slice(half * FFN_COLS, (half + 1) * FFN_COLS)
            u_s[slot, 8:tm + 8, lanes] = u
            u_s[slot, 0:8, lanes] = carry_ref[:, col0:col0 + FFN_COLS]
            carry_ref[:, col0:col0 + FFN_COLS] = u[tm - 8:tm]

    def conv_act(c):
        slot = c % 2

        def conv(r0, col, wcol):
            w = u_s[slot, r0:r0 + _PIECE_ROWS + 8, col:col + _PIECE_LANES]
            lanes = slice(wcol, wcol + _PIECE_LANES)
            r1 = pltpu.roll(w, 1, 0)[8:]
            r2 = pltpu.roll(w, 2, 0)[8:]
            return cb_ref[:, lanes] + cw_ref[0:1, lanes] * r2 + cw_ref[1:2, lanes] * r1 + cw_ref[2:3, lanes] * w[8:]

        for r0 in range(0, tm, _PIECE_ROWS):
            for col in range(0, FFN_COLS, _PIECE_LANES):
                gate = conv(r0, col, c * FFN_COLS + col)
                half_val = conv(r0, FFN_COLS + col, D_FF + c * FFN_COLS + col)
                act_s[slot, r0:r0 + _PIECE_ROWS, col:col + _PIECE_LANES] = (
                    _two_gelu_tanh(gate) * half_val).astype(_BF16)

    acc = None
    up(0)
    for c in range(_N_FF_STEPS):
        if c + 1 < _N_FF_STEPS:
            up(c + 1)
        conv_act(c)
        d = jnp.dot(act_s[c % 2], wdn_ref[c], preferred_element_type=_F32)
        if c == 0:
            out_ref[...] = d
        else:
            out_ref[...] += d
    out_ref[...] = x1_ref[...] + _rms_norm(out_ref[...], gfo_ref[...])


def _out_ffn(attn, rw, x2, wo, g_mix_post, g_ffn_pre, g_ffn_post, wup, cw, cb, wdn, batch, seq):
    tm = FFN_ROWS
    nt = seq // tm
    rows = lambda n: pl.BlockSpec((tm, n), lambda b, t: (b * nt + t, 0))
    vec = pl.BlockSpec((1, D_MODEL), lambda b, t: (0, 0))
    const = lambda a: pl.BlockSpec(a.shape, lambda b, t: (0,) * a.ndim, pipeline_mode=pl.Buffered(1))
    return pl.pallas_call(
        _out_ffn_kernel,
        grid=(batch, nt),
        in_specs=[rows(D_ATTN), rows(D_RWKV), rows(D_MODEL), const(wo), vec, vec, vec,
                  const(wup), const(cw), const(cb), const(wdn)],
        out_specs=rows(D_MODEL),
        out_shape=jax.ShapeDtypeStruct((batch * seq, D_MODEL), _F32),
        scratch_shapes=[
            pltpu.VMEM((8, 2 * D_FF), _F32),
            pltpu.VMEM((tm, D_MODEL), _BF16),
            pltpu.VMEM((tm, D_MODEL), _F32),
            pltpu.VMEM((2, tm + 8, 2 * FFN_COLS), _F32),
            pltpu.VMEM((2, tm, FFN_COLS), _BF16),
        ],
        compiler_params=pltpu.CompilerParams(
            dimension_semantics=("arbitrary", "arbitrary"), vmem_limit_bytes=VMEM_LIMIT),
        name="out_ffn",
    )(attn, rw, x2, wo, g_mix_post, g_ffn_pre, g_ffn_post, wup, cw, cb, wdn)


def _t5_bucket_table():
    rel = (np.arange(BLOCK)[:, None] + BLOCK) - np.arange(2 * BLOCK)[None, :]
    n = np.maximum(rel, 0)
    max_exact = N_BUCKETS // 2
    large = max_exact + (np.log(np.maximum(n, 1).astype(np.float32) / np.float32(max_exact))
                         / np.float32(math.log(MAX_DISTANCE / max_exact))
                         * np.float32(N_BUCKETS - max_exact)).astype(np.int32)
    large = np.minimum(large, N_BUCKETS - 1)
    return np.where(n < max_exact, n, large).astype(np.int32)


def _row(a):
    return a.reshape(1, -1)


def _in_proj_stage(x2, norm_mix_pre, w_in, shift_mix, seq):
    return _in_proj(x2, _row(norm_mix_pre), w_in.astype(_BF16), _row(shift_mix), seq)


def _rwkv_stage(p, w0, w_decay_up, a0, w_iclr_up, w_gate_up, k_k, k_a, r_k, ln_x_g, ln_x_b, batch, seq):
    zeros = jnp.zeros((LORA_DECAY, D_RWKV), _F32)
    w_lora = jnp.concatenate([jnp.concatenate([w_decay_up, zeros], axis=1),
                              jnp.concatenate([zeros, w_iclr_up], axis=1)], axis=0).astype(_BF16)
    hid = jnp.arange(D_RWKV) // HEAD_DIM
    ones_bd = (hid[:, None] == hid[None, :]).astype(_BF16)
    tri = (jnp.arange(CHUNK)[:, None] >= jnp.arange(CHUNK)[None, :]).astype(_BF16)
    return _rwkv(p, w_lora, w_gate_up.astype(_BF16), ones_bd, tri, _row(w0), _row(a0), _row(k_k),
                 _row(k_a), _row(r_k), _row(ln_x_g), _row(ln_x_b), batch, seq)


def _ffn_stage(attn, rw, x2, norm_mix_post, norm_ffn_pre, norm_ffn_post, w_out, w_ffn_up, conv_w, conv_b,
               w_ffn_down, batch, seq):
    half_val = jnp.concatenate([jnp.ones((D_FF,), _F32), jnp.full((D_FF,), 0.5, _F32)])
    cw = jnp.pad(conv_w * half_val, ((0, 8 - conv_w.shape[0]), (0, 0)))
    cb = _row(conv_b * half_val)
    wdn = w_ffn_down.reshape(_N_FF_STEPS, FFN_COLS, D_MODEL).astype(_BF16)
    return _out_ffn(attn, rw, x2, w_out.astype(_BF16), _row(norm_mix_post), _row(norm_ffn_pre),
                    _row(norm_ffn_post), w_ffn_up.astype(_BF16), cw, cb, wdn, batch, seq)


def _layer(x2, batch, seq, norm_mix_pre, norm_mix_post, norm_ffn_pre, norm_ffn_post, w_in, bias, sinks,
           shift_mix, w0, w_decay_up, a0, w_iclr_up, w_gate_up, k_k, k_a, r_k, ln_x_g, ln_x_b,
           w_out, w_ffn_up, conv_w, conv_b, w_ffn_down):
    q, kd, vd, p = _in_proj_stage(x2, norm_mix_pre, w_in, shift_mix, seq)
    attn = _attention(sinks, q, kd, vd, bias, batch, seq)
    rw = _rwkv_stage(p, w0, w_decay_up, a0, w_iclr_up, w_gate_up, k_k, k_a, r_k, ln_x_g, ln_x_b,
                     batch, seq)
    return _ffn_stage(attn, rw, x2, norm_mix_post, norm_ffn_pre, norm_ffn_post, w_out, w_ffn_up, conv_w, conv_b,
                      w_ffn_down, batch, seq)


def kernel(x, norm_mix_pre, norm_mix_post, norm_ffn_pre, norm_ffn_post, w_in, rel_bias, sinks, rwkv_shift_mix, w0, w_decay_up, a0, w_iclr_up, w_gate_up, k_k, k_a, r_k, ln_x_g, ln_x_b, w_out, w_ffn_up, conv_w, conv_b, w_ffn_down):
    batch, seq, _ = x.shape
    depth = w_in.shape[0]
    bias = _bias_table(rel_bias.astype(_F32), jnp.asarray(_t5_bucket_table()))
    bias = bias.reshape(N_Q_HEADS // 2, 2 * BLOCK, 2 * BLOCK)
    x2 = x.reshape(batch * seq, D_MODEL)
    for l in range(depth):
        x2 = _layer(x2, batch, seq, norm_mix_pre[l], norm_mix_post[l], norm_ffn_pre[l], norm_ffn_post[l], w_in[l],
                    bias, sinks[l], rwkv_shift_mix[l], w0[l], w_decay_up[l], a0[l], w_iclr_up[l], w_gate_up[l],
                    k_k[l], k_a[l], r_k[l].reshape(-1), ln_x_g[l], ln_x_b[l], w_out[l], w_ffn_up[l], conv_w[l],
                    conv_b[l], w_ffn_down[l])
    return x2.reshape(batch, seq, D_MODEL)
```

```python
import functools
import math

import jax
import jax.numpy as jnp
import numpy as np
from jax import lax
from jax.experimental import pallas as pl
from jax.experimental.pallas import tpu as pltpu

D_MODEL = 1024
HEAD_DIM = 64
D_ATTN = 512
D_RWKV = 512
N_Q_HEADS = 8
N_KV_HEADS = 2
D_KV = 128
WINDOW = 128
BLOCK = 128
N_BUCKETS = 32
MAX_DISTANCE = 128
LORA_DECAY = 64
LORA_ICLR = 64
LORA_GATE = 128
RWKV_COLS = 3 * D_RWKV + LORA_DECAY + LORA_ICLR + LORA_GATE
D_FF = 4 * D_MODEL
NORM_EPS = 1e-6
GN_EPS = 64e-5
NEG_INF = -1e30

IN_PROJ_ROWS = 512
ATTN_BLOCKS = 4
RWKV_ROWS = 512
CHUNK = 64
GROUP_LANES = 256
FFN_ROWS = 512
FFN_COLS = 512
VMEM_LIMIT = 56 * 1024 * 1024

_BF16 = jnp.bfloat16
_F32 = jnp.float32


def _dot(a, b):
    return jnp.dot(a.astype(_BF16), b.astype(_BF16), preferred_element_type=_F32)


def _dot_nt(a, b):
    return lax.dot_general(a.astype(_BF16), b.astype(_BF16), (((1,), (1,)), ((), ())),
                           preferred_element_type=_F32)


def _rms_norm(x, g):
    return x * lax.rsqrt(jnp.mean(x * x, axis=-1, keepdims=True) + NORM_EPS) * g


def _sigmoid(x):
    return 0.5 + 0.5 * jnp.tanh(0.5 * x)


def _bias_kernel(rb_ref, bucket_ref, out_ref):
    h = pl.program_id(0)
    bucket = bucket_ref[...]
    qi = lax.broadcasted_iota(jnp.int32, (BLOCK, 2 * BLOCK), 0)
    kj = lax.broadcasted_iota(jnp.int32, (BLOCK, 2 * BLOCK), 1)
    rel = qi + BLOCK - kj
    acc = jnp.zeros((BLOCK, 2 * BLOCK), _F32)
    for b in range(N_BUCKETS):
        acc = jnp.where(bucket == b, rb_ref[b, h], acc)
    out_ref[0] = jnp.where((rel >= 0) & (rel < WINDOW), acc, NEG_INF)


def _bias_table(rel_bias, bucket):
    return pl.pallas_call(
        _bias_kernel,
        grid=(N_Q_HEADS,),
        in_specs=[
            pl.BlockSpec(memory_space=pltpu.SMEM),
            pl.BlockSpec((BLOCK, 2 * BLOCK), lambda h: (0, 0)),
        ],
        out_specs=pl.BlockSpec((1, BLOCK, 2 * BLOCK), lambda h: (h, 0, 0)),
        out_shape=jax.ShapeDtypeStruct((N_Q_HEADS, BLOCK, 2 * BLOCK), _F32),
        name="bias_table",
    )(rel_bias, bucket)


_QW = D_ATTN
_KW = 2 * D_KV
_IN_COLS = D_ATTN + 2 * D_KV + RWKV_COLS


def _dup_heads(t):
    low = lax.broadcasted_iota(jnp.int32, (1, D_KV), 1) < HEAD_DIM
    swapped = pltpu.roll(t, HEAD_DIM, 1)
    return jnp.concatenate([jnp.where(low, t, swapped), jnp.where(low, swapped, t)], axis=1)


def _in_proj_kernel(tiles_per_seq, x_ref, g_ref, w_ref, mix_ref, q_ref, k_ref, v_ref, p_ref, carry_ref):
    @pl.when(pl.program_id(0) % tiles_per_seq == 0)
    def _():
        carry_ref[...] = jnp.zeros_like(carry_ref)

    h = _rms_norm(x_ref[...], g_ref[...]).astype(_BF16)
    q_ref[...] = (_dot(h, w_ref[:, 0:D_ATTN]) * (HEAD_DIM ** -0.5)).astype(_BF16)
    k_ref[...] = _dup_heads(_dot(h, w_ref[:, D_ATTN:D_ATTN + D_KV])).astype(_BF16)
    v_ref[...] = _dup_heads(_dot(h, w_ref[:, D_ATTN + D_KV:D_ATTN + 2 * D_KV])).astype(_BF16)
    p = _dot(h, w_ref[:, D_ATTN + 2 * D_KV:_IN_COLS])
    rolled = pltpu.roll(p, 1, 0)
    row8 = lax.broadcasted_iota(jnp.int32, (8, 1), 0)
    head = jnp.where(row8 == 0, carry_ref[...], rolled[0:8])
    carry_ref[...] = rolled[0:8]
    p_prev = jnp.concatenate([head, rolled[8:]], axis=0)
    p_ref[...] = p + (p_prev - p) * mix_ref[...]


def _in_proj(x2, g, w, mix, seq):
    t = x2.shape[0]
    tm = IN_PROJ_ROWS
    return pl.pallas_call(
        functools.partial(_in_proj_kernel, seq // tm),
        grid=(t // tm,),
        in_specs=[
            pl.BlockSpec((tm, D_MODEL), lambda i: (i, 0)),
            pl.BlockSpec((1, D_MODEL), lambda i: (0, 0)),
            pl.BlockSpec((D_MODEL, _IN_COLS), lambda i: (0, 0)),
            pl.BlockSpec((1, RWKV_COLS), lambda i: (0, 0)),
        ],
        scratch_shapes=[pltpu.VMEM((8, RWKV_COLS), _F32)],
        out_specs=[
            pl.BlockSpec((tm, _QW), lambda i: (i, 0)),
            pl.BlockSpec((tm, _KW), lambda i: (i, 0)),
            pl.BlockSpec((tm, _KW), lambda i: (i, 0)),
            pl.BlockSpec((tm, RWKV_COLS), lambda i: (i, 0)),
        ],
        out_shape=[
            jax.ShapeDtypeStruct((t, _QW), _BF16),
            jax.ShapeDtypeStruct((t, _KW), _BF16),
            jax.ShapeDtypeStruct((t, _KW), _BF16),
            jax.ShapeDtypeStruct((t, RWKV_COLS), _F32),
        ],
        compiler_params=pltpu.CompilerParams(
            dimension_semantics=("arbitrary",), vmem_limit_bytes=VMEM_LIMIT),
        name="in_proj",
    )(x2, g, w, mix)


def _attn_kernel(sink_ref, q_ref, kc_ref, kp_ref, vc_ref, vp_ref, bias_ref, o_ref):
    step = pl.program_id(1)
    col = lax.broadcasted_iota(jnp.int32, (1, 2 * BLOCK), 1)
    first_valid = col >= jnp.where(step == 0, BLOCK, 0)
    lane = lax.broadcasted_iota(jnp.int32, (1, 2 * HEAD_DIM), 1)
    low = lane < HEAD_DIM
    top = lax.broadcasted_iota(jnp.int32, (2 * BLOCK, 1), 0) < BLOCK
    chains = [(blk, pair) for blk in range(ATTN_BLOCKS) for pair in range(N_Q_HEADS // 2)]

    def band(cur_ref, prev_ref, blk, pair):
        j = (2 * pair) // (N_Q_HEADS // N_KV_HEADS)
        kv_lanes = slice(2 * HEAD_DIM * j, 2 * HEAD_DIM * (j + 1))
        prev = prev_ref[:, kv_lanes] if blk == 0 else cur_ref[BLOCK * (blk - 1):BLOCK * blk, kv_lanes]
        return jnp.concatenate([prev, cur_ref[BLOCK * blk:BLOCK * (blk + 1), kv_lanes]], axis=0)

    s = []
    for blk, pair in chains:
        qp = q_ref[BLOCK * blk:BLOCK * (blk + 1), 2 * HEAD_DIM * pair:2 * HEAD_DIM * (pair + 1)]
        zero = jnp.zeros_like(qp)
        q_st = jnp.concatenate([jnp.where(low, qp, zero), jnp.where(low, zero, qp)], axis=0)
        sc = _dot_nt(q_st, band(kc_ref, kp_ref, blk, pair)) + bias_ref[pair]
        s.append(jnp.where(first_valid, sc, NEG_INF) if blk == 0 else sc)
    sink = [jnp.where(top, sink_ref[2 * pair], sink_ref[2 * pair + 1]) for _, pair in chains]
    m = [jnp.maximum(jnp.max(x, axis=-1, keepdims=True), k) for x, k in zip(s, sink)]
    e = [jnp.exp(x - mm) for x, mm in zip(s, m)]
    den = [jnp.sum(x, axis=-1, keepdims=True) + jnp.exp(k - mm) for x, k, mm in zip(e, sink, m)]
    pv = [_dot(x, band(vc_ref, vp_ref, blk, pair)) for x, (blk, pair) in zip(e, chains)]
    for (blk, pair), x, d in zip(chains, pv, den):
        o = x / d
        o_ref[BLOCK * blk:BLOCK * (blk + 1), 2 * HEAD_DIM * pair:2 * HEAD_DIM * (pair + 1)] = jnp.where(
            low, o[0:BLOCK], o[BLOCK:2 * BLOCK]).astype(o_ref.dtype)


def _attention(sinks, q, kd, vd, bias, batch, seq):
    rows = ATTN_BLOCKS * BLOCK
    nt = seq // rows
    cur = lambda b, n: (b * nt + n, 0)
    prev = lambda b, n: (jnp.maximum((b * nt + n) * ATTN_BLOCKS - 1, 0), 0)
    return pl.pallas_call(
        _attn_kernel,
        grid=(batch, nt),
        in_specs=[
            pl.BlockSpec(memory_space=pltpu.SMEM),
            pl.BlockSpec((rows, _QW), cur),
            pl.BlockSpec((rows, _KW), cur),
            pl.BlockSpec((BLOCK, _KW), prev),
            pl.BlockSpec((rows, _KW), cur),
            pl.BlockSpec((BLOCK, _KW), prev),
            pl.BlockSpec((N_Q_HEADS // 2, 2 * BLOCK, 2 * BLOCK), lambda b, n: (0, 0, 0)),
        ],
        out_specs=pl.BlockSpec((rows, D_ATTN), cur),
        out_shape=jax.ShapeDtypeStruct((batch * seq, D_ATTN), _BF16),
        compiler_params=pltpu.CompilerParams(
            dimension_semantics=("arbitrary", "arbitrary"), vmem_limit_bytes=VMEM_LIMIT),
        name="attention",
    )(sinks, q, kd, kd, vd, vd, bias)


_HEADS_PER_GROUP = GROUP_LANES // HEAD_DIM
_N_GROUPS = D_RWKV // GROUP_LANES
_CHUNKS_PER_STEP = 8


def _split3(x):
    hi = x.astype(_BF16)
    r1 = x - hi.astype(_F32)
    mid = r1.astype(_BF16)
    lo = (r1 - mid.astype(_F32)).astype(_BF16)
    return hi, mid, lo


def _rwkv_kernel(p_ref, wl_ref, wg_ref, ones_ref, tri_ref, w0_ref, a0_ref, kk_ref, ka_ref,
                 rk_ref, lng_ref, lnb_ref, o_ref,
                 st_ref, q_s, al_s, kh_s, bh_s, kb_s, bb_s, v_s, plast_s,
                 qh_s, oi_s, lr_s, dcol_s, n_s, g_s, bo_s, o_s):
    tt = p_ref.shape[0]
    n_chunks = tt // CHUNK

    @pl.when(pl.program_id(1) == 0)
    def _():
        st_ref[...] = jnp.zeros_like(st_ref)

    r = p_ref[:, 0:D_RWKV]
    k = p_ref[:, D_RWKV:2 * D_RWKV]
    v = p_ref[:, 2 * D_RWKV:3 * D_RWKV]
    z = p_ref[:, 3 * D_RWKV:3 * D_RWKV + LORA_DECAY + LORA_ICLR]
    zg = p_ref[:, 3 * D_RWKV + LORA_DECAY + LORA_ICLR:RWKV_COLS]

    lane128 = lax.broadcasted_iota(jnp.int32, (1, LORA_DECAY + LORA_ICLR), 1)
    zin = jnp.where(lane128 < LORA_DECAY, jnp.tanh(z), z)
    dl = _dot(zin, wl_ref[...])
    wpre = w0_ref[...] + dl[:, 0:D_RWKV]
    lw = (-math.exp(-0.5)) * _sigmoid(wpre)
    a = _sigmoid(a0_ref[...] + dl[:, D_RWKV:2 * D_RWKV])
    g_s[...] = _dot(_sigmoid(zg), wg_ref[...])

    ones_bd = ones_ref[...]
    kk = k * kk_ref[...]
    kk = kk * lax.rsqrt(jnp.maximum(_dot(kk * kk, ones_bd), 1e-24))
    kp = k * (1.0 + (a - 1.0) * ka_ref[...])
    be = kk * a
    bo_s[...] = _dot(r * kp * rk_ref[...], ones_bd) * v
    v_s[...] = v.astype(_BF16)

    tri = tri_ref[...]
    cums, lasts = [], []
    for c in range(n_chunks):
        hi, mid, lo = _split3(lw[c * CHUNK:(c + 1) * CHUNK])
        cum_c = _dot(tri, hi) + _dot(tri, mid) + _dot(tri, lo)
        p_last = jnp.exp(cum_c[CHUNK - 1:CHUNK, :])
        plast_s[c:c + 1, :] = p_last
        cums.append(cum_c)
        lasts.append(jnp.broadcast_to(p_last, (CHUNK, D_RWKV)))
    cum = jnp.concatenate(cums, axis=0)
    q_s[...] = (r * jnp.exp(cum)).astype(_BF16)
    al_s[...] = (-kk * jnp.exp(cum - lw)).astype(_BF16)
    inv = jnp.exp(-cum)
    kh_s[...] = (kp * inv).astype(_BF16)
    bh_s[...] = (be * inv).astype(_BF16)
    rest = jnp.concatenate(lasts, axis=0) * inv
    kb_s[...] = (kp * rest).astype(_BF16)
    bb_s[...] = (be * rest).astype(_BF16)

    gl = GROUP_LANES
    lane = lax.broadcasted_iota(jnp.int32, (1, gl), 1)
    head_masks = [(lane >= HEAD_DIM * h) & (lane < HEAD_DIM * (h + 1)) for h in range(_HEADS_PER_GROUP)]
    s_local = lane & (HEAD_DIM - 1)
    trow = lax.broadcasted_iota(jnp.int32, (CHUNK, 1), 0)
    strict = s_local < trow
    incl = s_local <= trow
    eye_tok = jnp.where(s_local == trow, 1.0, 0.0).astype(_F32)
    grow = lax.broadcasted_iota(jnp.int32, (gl, 1), 0)
    same_head = (grow >> 6) == (lane >> 6)

    def stack(x):
        xb = x.astype(_BF16)
        zero = jnp.zeros_like(xb)
        return jnp.concatenate([jnp.where(m, xb, zero) for m in head_masks], axis=0)

    def intra_body(it, carry):
        chains = [(it * _CHUNKS_PER_STEP + u, gi) for u in range(_CHUNKS_PER_STEP) for gi in range(_N_GROUPS)]
        idx = [(pl.ds(pl.multiple_of(c * CHUNK, CHUNK), CHUNK), slice(gl * gi, gl * (gi + 1))) for c, gi in chains]
        al = [al_s[r, s] for r, s in idx]
        q = [q_s[r, s] for r, s in idx]
        v = [v_s[r, s] for r, s in idx]
        gram = [_dot_nt(jnp.concatenate([a, b], axis=0),
                        jnp.concatenate([stack(kh_s[r, s]), stack(bh_s[r, s])], axis=0))
                for a, b, (r, s) in zip(al, q, idx)]
        a_ak = [jnp.where(strict, g[0:CHUNK, 0:gl], 0.0) for g in gram]
        a_ab = [jnp.where(strict, g[0:CHUNK, gl:2 * gl], 0.0) for g in gram]
        a_qk = [jnp.where(incl, g[CHUNK:2 * CHUNK, 0:gl], 0.0) for g in gram]
        a_qb = [jnp.where(incl, g[CHUNK:2 * CHUNK, gl:2 * gl], 0.0) for g in gram]
        akv = [_dot(a, stack(b)) for a, b in zip(a_ak, v)]
        pw = [_dot(n, stack(n)) for n in a_ab]
        tinv = [eye_tok + n for n in a_ab]
        for level in range(1, 6):
            rhs = [stack(x) for x in pw]
            if level < 5:
                res = [_dot(jnp.concatenate([x, t], axis=0), w) for x, t, w in zip(pw, tinv, rhs)]
                pw = [x[0:CHUNK] for x in res]
                tinv = [t + x[CHUNK:2 * CHUNK] for t, x in zip(tinv, res)]
            else:
                tinv = [t + _dot(t, w) for t, w in zip(tinv, rhs)]
        y = [_dot(t, jnp.concatenate([stack(a), stack(b)], axis=1)) for t, a, b in zip(tinv, al, akv)]
        al_hat = [x[:, 0:gl] for x in y]
        u_v = [x[:, gl:2 * gl] for x in y]
        qh = [_dot(a, stack(b)) for a, b in zip(a_qb, al_hat)]
        oi = [_dot(jnp.concatenate([a, b], axis=1), jnp.concatenate([stack(c_), stack(d)], axis=0))
              for a, b, c_, d in zip(a_qk, a_qb, v, u_v)]
        tr = [jnp.concatenate([kb_s[r, s].astype(_F32), bb_s[r, s].astype(_F32),
                               jnp.broadcast_to(plast_s[pl.ds(c, 1), s], (2 * CHUNK, gl))], axis=0).T
              for (c, gi), (r, s) in zip(chains, idx)]
        upd = [_dot(t[:, 0:2 * CHUNK],
                    jnp.concatenate([jnp.concatenate([c_, jnp.zeros_like(c_)], axis=1),
                                     jnp.concatenate([d, e], axis=1).astype(_BF16)], axis=0))
               for t, c_, d, e in zip(tr, v, u_v, al_hat)]
        for (c, gi), (r, s), q_i, qh_i, oi_i, upd_i, t in zip(chains, idx, q, qh, oi, upd, tr):
            qh_s[r, s] = (q_i.astype(_F32) + qh_i).astype(_BF16)
            oi_s[r, s] = oi_i
            n_s[c, gi] = jnp.where(same_head, upd_i[:, 0:gl], 0.0)
            lr_s[c, gi] = jnp.where(same_head, upd_i[:, gl:2 * gl], 0.0).astype(_BF16)
            dcol_s[c, gi] = t[:, 2 * CHUNK:4 * CHUNK]
        return carry

    lax.fori_loop(0, n_chunks // _CHUNKS_PER_STEP, intra_body, 0)

    def state_body(c, carry):
        rows = pl.ds(pl.multiple_of(c * CHUNK, CHUNK), CHUNK)
        sls = [slice(gl * gi, gl * (gi + 1)) for gi in range(_N_GROUPS)]
        st = [st_ref[gi] for gi in range(_N_GROUPS)]
        sb = [x.astype(_BF16) for x in st]
        new = [_dot(lr_s[c, gi], sb[gi]) for gi in range(_N_GROUPS)]
        out = [_dot(qh_s[rows, sls[gi]], sb[gi]) for gi in range(_N_GROUPS)]
        for gi in range(_N_GROUPS):
            dcol = dcol_s[c, gi]
            st_ref[gi] = jnp.concatenate([dcol] * (gl // (2 * CHUNK)), axis=1) * st[gi] + new[gi] + n_s[c, gi]
            o_s[rows, sls[gi]] = out[gi] + oi_s[rows, sls[gi]]
        return carry

    lax.fori_loop(0, n_chunks, state_body, 0)

    o = o_s[...]
    mu = _dot(o, ones_bd) * (1.0 / HEAD_DIM)
    d = o - mu
    var = _dot(d * d, ones_bd) * (1.0 / HEAD_DIM)
    on = d * lax.rsqrt(var + GN_EPS) * lng_ref[...] + lnb_ref[...]
    o_ref[...] = ((on + bo_s[...]) * g_s[...]).astype(o_ref.dtype)


def _rwkv(p, wl, wg, ones_bd, tri, w0, a0, k_k, k_a, r_k, ln_g, ln_b, batch, seq):
    tt = RWKV_ROWS
    nt = seq // tt
    vec = lambda n: pl.BlockSpec((1, n), lambda b, t: (0, 0))
    full = lambda a: pl.BlockSpec(a.shape, lambda b, t: (0,) * a.ndim)
    tile = lambda dt: pltpu.VMEM((tt, D_RWKV), dt)
    per_chunk = lambda dt: pltpu.VMEM((tt // CHUNK, _N_GROUPS, GROUP_LANES, GROUP_LANES), dt)
    return pl.pallas_call(
        _rwkv_kernel,
        grid=(batch, nt),
        in_specs=[
            pl.BlockSpec((tt, RWKV_COLS), lambda b, t: (b * nt + t, 0)),
            full(wl), full(wg), full(ones_bd), full(tri),
            vec(D_RWKV), vec(D_RWKV), vec(D_RWKV), vec(D_RWKV), vec(D_RWKV), vec(D_RWKV), vec(D_RWKV),
        ],
        out_specs=pl.BlockSpec((tt, D_RWKV), lambda b, t: (b * nt + t, 0)),
        out_shape=jax.ShapeDtypeStruct((batch * seq, D_RWKV), _BF16),
        scratch_shapes=[
            pltpu.VMEM((_N_GROUPS, GROUP_LANES, GROUP_LANES), _F32),
            tile(_BF16), tile(_BF16), tile(_BF16), tile(_BF16), tile(_BF16), tile(_BF16), tile(_BF16),
            pltpu.VMEM((tt // CHUNK, D_RWKV), _F32),
            tile(_BF16), tile(_F32), per_chunk(_BF16),
            pltpu.VMEM((tt // CHUNK, _N_GROUPS, GROUP_LANES, 2 * CHUNK), _F32), per_chunk(_F32),
            tile(_F32), tile(_F32), tile(_F32),
        ],
        compiler_params=pltpu.CompilerParams(
            dimension_semantics=("arbitrary", "arbitrary"), vmem_limit_bytes=VMEM_LIMIT),
        name="rwkv",
    )(p, wl, wg, ones_bd, tri, w0, a0, k_k, k_a, r_k, ln_g, ln_b)


_N_FF_STEPS = D_FF // FFN_COLS
_PIECE_ROWS = 64
_PIECE_LANES = 128


_GELU_C1 = 0.7978845608028654
_GELU_C2 = _GELU_C1 * 0.044715


def _two_gelu_tanh(x):
    return x + x * jnp.tanh(x * (_GELU_C1 + _GELU_C2 * (x * x)))


def _out_ffn_kernel(attn_ref, rw_ref, x_ref, wo_ref, gmp_ref, gfp_ref, gfo_ref,
                    wup_ref, cw_ref, cb_ref, wdn_ref, out_ref,
                    carry_ref, h_ref, x1_ref, u_s, act_s):
    tm = x_ref.shape[0]

    @pl.when(pl.program_id(1) == 0)
    def _():
        carry_ref[...] = jnp.zeros_like(carry_ref)

    mix = _dot(attn_ref[...], wo_ref[0:D_ATTN, :]) + _dot(rw_ref[...], wo_ref[D_ATTN:D_MODEL, :])
    x1 = x_ref[...] + _rms_norm(mix, gmp_ref[...])
    x1_ref[...] = x1
    h_ref[...] = _rms_norm(x1, gfp_ref[...]).astype(_BF16)

    def up(c):
        slot = c % 2
        h = h_ref[...]
        for half, col0 in enumerate((c * FFN_COLS, D_FF + c * FFN_COLS)):
            u = jnp.dot(h, wup_ref[:, col0:col0 + FFN_COLS], preferred_element_type=_F32)
            lanes = slice(half * FFN_COLS, (half + 1) * FFN_COLS)
            u_s[slot, 8:tm + 8, lanes] = u
            u_s[slot, 0:8, lanes] = carry_ref[:, col0:col0 + FFN_COLS]
            carry_ref[:, col0:col0 + FFN_COLS] = u[tm - 8:tm]

    def conv_act(c):
        slot = c % 2

        def conv(r0, col, wcol):
            w = u_s[slot, r0:r0 + _PIECE_ROWS + 8, col:col + _PIECE_LANES]
            lanes = slice(wcol, wcol + _PIECE_LANES)
            r1 = pltpu.roll(w, 1, 0)[8:]
            r2 = pltpu.roll(w, 2, 0)[8:]
            return cb_ref[:, lanes] + cw_ref[0:1, lanes] * r2 + cw_ref[1:2, lanes] * r1 + cw_ref[2:3, lanes] * w[8:]

        for r0 in range(0, tm, _PIECE_ROWS):
            for col in range(0, FFN_COLS, _PIECE_LANES):
                gate = conv(r0, col, c * FFN_COLS + col)
                half_val = conv(r0, FFN_COLS + col, D_FF + c * FFN_COLS + col)
                act_s[slot, r0:r0 + _PIECE_ROWS, col:col + _PIECE_LANES] = (
                    _two_gelu_tanh(gate) * half_val).astype(_BF16)

    acc = None
    up(0)
    for c in range(_N_FF_STEPS):
        if c + 1 < _N_FF_STEPS:
            up(c + 1)
        conv_act(c)
        d = jnp.dot(act_s[c % 2], wdn_ref[c], preferred_element_type=_F32)
        if c == 0:
            out_ref[...] = d
        else:
            out_ref[...] += d
    out_ref[...] = x1_ref[...] + _rms_norm(out_ref[...], gfo_ref[...])


def _out_ffn(attn, rw, x2, wo, g_mix_post, g_ffn_pre, g_ffn_post, wup, cw, cb, wdn, batch, seq):
    tm = FFN_ROWS
    nt = seq // tm
    rows = lambda n: pl.BlockSpec((tm, n), lambda b, t: (b * nt + t, 0))
    vec = pl.BlockSpec((1, D_MODEL), lambda b, t: (0, 0))
    const = lambda a: pl.BlockSpec(a.shape, lambda b, t: (0,) * a.ndim, pipeline_mode=pl.Buffered(1))
    return pl.pallas_call(
        _out_ffn_kernel,
        grid=(batch, nt),
        in_specs=[rows(D_ATTN), rows(D_RWKV), rows(D_MODEL), const(wo), vec, vec, vec,
                  const(wup), const(cw), const(cb), const(wdn)],
        out_specs=rows(D_MODEL),
        out_shape=jax.ShapeDtypeStruct((batch * seq, D_MODEL), _F32),
        scratch_shapes=[
            pltpu.VMEM((8, 2 * D_FF), _F32),
            pltpu.VMEM((tm, D_MODEL), _BF16),
            pltpu.VMEM((tm, D_MODEL), _F32),
            pltpu.VMEM((2, tm + 8, 2 * FFN_COLS), _F32),
            pltpu.VMEM((2, tm, FFN_COLS), _BF16),
        ],
        compiler_params=pltpu.CompilerParams(
            dimension_semantics=("arbitrary", "arbitrary"), vmem_limit_bytes=VMEM_LIMIT),
        name="out_ffn",
    )(attn, rw, x2, wo, g_mix_post, g_ffn_pre, g_ffn_post, wup, cw, cb, wdn)


def _t5_bucket_table():
    rel = (np.arange(BLOCK)[:, None] + BLOCK) - np.arange(2 * BLOCK)[None, :]
    n = np.maximum(rel, 0)
    max_exact = N_BUCKETS // 2
    large = max_exact + (np.log(np.maximum(n, 1).astype(np.float32) / np.float32(max_exact))
                         / np.float32(math.log(MAX_DISTANCE / max_exact))
                         * np.float32(N_BUCKETS - max_exact)).astype(np.int32)
    large = np.minimum(large, N_BUCKETS - 1)
    return np.where(n < max_exact, n, large).astype(np.int32)


def _row(a):
    return a.reshape(1, -1)


def _in_proj_stage(x2, norm_mix_pre, w_in, shift_mix, seq):
    return _in_proj(x2, _row(norm_mix_pre), w_in.astype(_BF16), _row(shift_mix), seq)


def _rwkv_stage(p, w0, w_decay_up, a0, w_iclr_up, w_gate_up, k_k, k_a, r_k, ln_x_g, ln_x_b, batch, seq):
    zeros = jnp.zeros((LORA_DECAY, D_RWKV), _F32)
    w_lora = jnp.concatenate([jnp.concatenate([w_decay_up, zeros], axis=1),
                              jnp.concatenate([zeros, w_iclr_up], axis=1)], axis=0).astype(_BF16)
    hid = jnp.arange(D_RWKV) // HEAD_DIM
    ones_bd = (hid[:, None] == hid[None, :]).astype(_BF16)
    tri = (jnp.arange(CHUNK)[:, None] >= jnp.arange(CHUNK)[None, :]).astype(_BF16)
    return _rwkv(p, w_lora, w_gate_up.astype(_BF16), ones_bd, tri, _row(w0), _row(a0), _row(k_k),
                 _row(k_a), _row(r_k), _row(ln_x_g), _row(ln_x_b), batch, seq)


def _ffn_stage(attn, rw, x2, norm_mix_post, norm_ffn_pre, norm_ffn_post, w_out, w_ffn_up, conv_w, conv_b,
               w_ffn_down, batch, seq):
    half_val = jnp.concatenate([jnp.ones((D_FF,), _F32), jnp.full((D_FF,), 0.5, _F32)])
    cw = jnp.pad(conv_w * half_val, ((0, 8 - conv_w.shape[0]), (0, 0)))
    cb = _row(conv_b * half_val)
    wdn = w_ffn_down.reshape(_N_FF_STEPS, FFN_COLS, D_MODEL).astype(_BF16)
    return _out_ffn(attn, rw, x2, w_out.astype(_BF16), _row(norm_mix_post), _row(norm_ffn_pre),
                    _row(norm_ffn_post), w_ffn_up.astype(_BF16), cw, cb, wdn, batch, seq)


def _layer(x2, batch, seq, norm_mix_pre, norm_mix_post, norm_ffn_pre, norm_ffn_post, w_in, bias, sinks,
           shift_mix, w0, w_decay_up, a0, w_iclr_up, w_gate_up, k_k, k_a, r_k, ln_x_g, ln_x_b,
           w_out, w_ffn_up, conv_w, conv_b, w_ffn_down):
    q, kd, vd, p = _in_proj_stage(x2, norm_mix_pre, w_in, shift_mix, seq)
    attn = _attention(sinks, q, kd, vd, bias, batch, seq)
    rw = _rwkv_stage(p, w0, w_decay_up, a0, w_iclr_up, w_gate_up, k_k, k_a, r_k, ln_x_g, ln_x_b,
                     batch, seq)
    return _ffn_stage(attn, rw, x2, norm_mix_post, norm_ffn_pre, norm_ffn_post, w_out, w_ffn_up, conv_w, conv_b,
                      w_ffn_down, batch, seq)


def kernel(x, norm_mix_pre, norm_mix_post, norm_ffn_pre, norm_ffn_post, w_in, rel_bias, sinks, rwkv_shift_mix, w0, w_decay_up, a0, w_iclr_up, w_gate_up, k_k, k_a, r_k, ln_x_g, ln_x_b, w_out, w_ffn_up, conv_w, conv_b, w_ffn_down):
    batch, seq, _ = x.shape
    depth = w_in.shape[0]
    bias = _bias_table(rel_bias.astype(_F32), jnp.asarray(_t5_bucket_table()))
    bias = bias.reshape(N_Q_HEADS // 2, 2 * BLOCK, 2 * BLOCK)
    x2 = x.reshape(batch * seq, D_MODEL)
    for l in range(depth):
        x2 = _layer(x2, batch, seq, norm_mix_pre[l], norm_mix_post[l], norm_ffn_pre[l], norm_ffn_post[l], w_in[l],
                    bias, sinks[l], rwkv_shift_mix[l], w0[l], w_decay_up[l], a0[l], w_iclr_up[l], w_gate_up[l],
                    k_k[l], k_a[l], r_k[l].reshape(-1), ln_x_g[l], ln_x_b[l], w_out[l], w_ffn_up[l], conv_w[l],
                    conv_b[l], w_ffn_down[l])
    return x2.reshape(batch, seq, D_MODEL)
```

```python
import functools
import math

import jax
import jax.numpy as jnp
import numpy as np
from jax import lax
from jax.experimental import pallas as pl
from jax.experimental.pallas import tpu as pltpu

D_MODEL = 1024
HEAD_DIM = 64
D_ATTN = 512
D_RWKV = 512
N_Q_HEADS = 8
N_KV_HEADS = 2
D_KV = 128
WINDOW = 128
BLOCK = 128
N_BUCKETS = 32
MAX_DISTANCE = 128
LORA_DECAY = 64
LORA_ICLR = 64
LORA_GATE = 128
RWKV_COLS = 3 * D_RWKV + LORA_DECAY + LORA_ICLR + LORA_GATE
D_FF = 4 * D_MODEL
NORM_EPS = 1e-6
GN_EPS = 64e-5
NEG_INF = -1e30

IN_PROJ_ROWS = 512
ATTN_BLOCKS = 4
RWKV_ROWS = 512
CHUNK = 64
GROUP_LANES = 128
FFN_ROWS = 512
FFN_COLS = 512
VMEM_LIMIT = 56 * 1024 * 1024

_BF16 = jnp.bfloat16
_F32 = jnp.float32


def _dot(a, b):
    return jnp.dot(a.astype(_BF16), b.astype(_BF16), preferred_element_type=_F32)


def _dot_nt(a, b):
    return lax.dot_general(a.astype(_BF16), b.astype(_BF16), (((1,), (1,)), ((), ())),
                           preferred_element_type=_F32)


def _rms_norm(x, g):
    return x * lax.rsqrt(jnp.mean(x * x, axis=-1, keepdims=True) + NORM_EPS) * g


def _sigmoid(x):
    return 0.5 + 0.5 * jnp.tanh(0.5 * x)


def _bias_kernel(rb_ref, bucket_ref, out_ref):
    h = pl.program_id(0)
    bucket = bucket_ref[...]
    qi = lax.broadcasted_iota(jnp.int32, (BLOCK, 2 * BLOCK), 0)
    kj = lax.broadcasted_iota(jnp.int32, (BLOCK, 2 * BLOCK), 1)
    rel = qi + BLOCK - kj
    acc = jnp.zeros((BLOCK, 2 * BLOCK), _F32)
    for b in range(N_BUCKETS):
        acc = jnp.where(bucket == b, rb_ref[b, h], acc)
    out_ref[0] = jnp.where((rel >= 0) & (rel < WINDOW), acc, NEG_INF)


def _bias_table(rel_bias, bucket):
    return pl.pallas_call(
        _bias_kernel,
        grid=(N_Q_HEADS,),
        in_specs=[
            pl.BlockSpec(memory_space=pltpu.SMEM),
            pl.BlockSpec((BLOCK, 2 * BLOCK), lambda h: (0, 0)),
        ],
        out_specs=pl.BlockSpec((1, BLOCK, 2 * BLOCK), lambda h: (h, 0, 0)),
        out_shape=jax.ShapeDtypeStruct((N_Q_HEADS, BLOCK, 2 * BLOCK), _F32),
        name="bias_table",
    )(rel_bias, bucket)


_QW = D_ATTN
_KW = 2 * D_KV
_IN_COLS = D_ATTN + 2 * D_KV + RWKV_COLS


def _dup_heads(t):
    low = lax.broadcasted_iota(jnp.int32, (1, D_KV), 1) < HEAD_DIM
    swapped = pltpu.roll(t, HEAD_DIM, 1)
    return jnp.concatenate([jnp.where(low, t, swapped), jnp.where(low, swapped, t)], axis=1)


def _in_proj_kernel(tiles_per_seq, x_ref, g_ref, w_ref, mix_ref, q_ref, k_ref, v_ref, p_ref, carry_ref):
    @pl.when(pl.program_id(0) % tiles_per_seq == 0)
    def _():
        carry_ref[...] = jnp.zeros_like(carry_ref)

    h = _rms_norm(x_ref[...], g_ref[...]).astype(_BF16)
    q_ref[...] = (_dot(h, w_ref[:, 0:D_ATTN]) * (HEAD_DIM ** -0.5)).astype(_BF16)
    k_ref[...] = _dup_heads(_dot(h, w_ref[:, D_ATTN:D_ATTN + D_KV])).astype(_BF16)
    v_ref[...] = _dup_heads(_dot(h, w_ref[:, D_ATTN + D_KV:D_ATTN + 2 * D_KV])).astype(_BF16)
    p = _dot(h, w_ref[:, D_ATTN + 2 * D_KV:_IN_COLS])
    rolled = pltpu.roll(p, 1, 0)
    row8 = lax.broadcasted_iota(jnp.int32, (8, 1), 0)
    head = jnp.where(row8 == 0, carry_ref[...], rolled[0:8])
    carry_ref[...] = rolled[0:8]
    p_prev = jnp.concatenate([head, rolled[8:]], axis=0)
    p_ref[...] = p + (p_prev - p) * mix_ref[...]


def _in_proj(x2, g, w, mix, seq):
    t = x2.shape[0]
    tm = IN_PROJ_ROWS
    return pl.pallas_call(
        functools.partial(_in_proj_kernel, seq // tm),
        grid=(t // tm,),
        in_specs=[
            pl.BlockSpec((tm, D_MODEL), lambda i: (i, 0)),
            pl.BlockSpec((1, D_MODEL), lambda i: (0, 0)),
            pl.BlockSpec((D_MODEL, _IN_COLS), lambda i: (0, 0)),
            pl.BlockSpec((1, RWKV_COLS), lambda i: (0, 0)),
        ],
        scratch_shapes=[pltpu.VMEM((8, RWKV_COLS), _F32)],
        out_specs=[
            pl.BlockSpec((tm, _QW), lambda i: (i, 0)),
            pl.BlockSpec((tm, _KW), lambda i: (i, 0)),
            pl.BlockSpec((tm, _KW), lambda i: (i, 0)),
            pl.BlockSpec((tm, RWKV_COLS), lambda i: (i, 0)),
        ],
        out_shape=[
            jax.ShapeDtypeStruct((t, _QW), _BF16),
            jax.ShapeDtypeStruct((t, _KW), _BF16),
            jax.ShapeDtypeStruct((t, _KW), _BF16),
            jax.ShapeDtypeStruct((t, RWKV_COLS), _F32),
        ],
        compiler_params=pltpu.CompilerParams(
            dimension_semantics=("arbitrary",), vmem_limit_bytes=VMEM_LIMIT),
        name="in_proj",
    )(x2, g, w, mix)


def _attn_kernel(sink_ref, q_ref, kc_ref, kp_ref, vc_ref, vp_ref, bias_ref, o_ref):
    step = pl.program_id(1)
    col = lax.broadcasted_iota(jnp.int32, (1, 2 * BLOCK), 1)
    first_valid = col >= jnp.where(step == 0, BLOCK, 0)
    lane = lax.broadcasted_iota(jnp.int32, (1, 2 * HEAD_DIM), 1)
    low = lane < HEAD_DIM
    top = lax.broadcasted_iota(jnp.int32, (2 * BLOCK, 1), 0) < BLOCK
    chains = [(blk, pair) for blk in range(ATTN_BLOCKS) for pair in range(N_Q_HEADS // 2)]

    def band(cur_ref, prev_ref, blk, pair):
        j = (2 * pair) // (N_Q_HEADS // N_KV_HEADS)
        kv_lanes = slice(2 * HEAD_DIM * j, 2 * HEAD_DIM * (j + 1))
        prev = prev_ref[:, kv_lanes] if blk == 0 else cur_ref[BLOCK * (blk - 1):BLOCK * blk, kv_lanes]
        return jnp.concatenate([prev, cur_ref[BLOCK * blk:BLOCK * (blk + 1), kv_lanes]], axis=0)

    s = []
    for blk, pair in chains:
        qp = q_ref[BLOCK * blk:BLOCK * (blk + 1), 2 * HEAD_DIM * pair:2 * HEAD_DIM * (pair + 1)]
        zero = jnp.zeros_like(qp)
        q_st = jnp.concatenate([jnp.where(low, qp, zero), jnp.where(low, zero, qp)], axis=0)
        sc = _dot_nt(q_st, band(kc_ref, kp_ref, blk, pair)) + bias_ref[pair]
        s.append(jnp.where(first_valid, sc, NEG_INF) if blk == 0 else sc)
    sink = [jnp.where(top, sink_ref[2 * pair], sink_ref[2 * pair + 1]) for _, pair in chains]
    m = [jnp.maximum(jnp.max(x, axis=-1, keepdims=True), k) for x, k in zip(s, sink)]
    e = [jnp.exp(x - mm) for x, mm in zip(s, m)]
    den = [jnp.sum(x, axis=-1, keepdims=True) + jnp.exp(k - mm) for x, k, mm in zip(e, sink, m)]
    pv = [_dot(x, band(vc_ref, vp_ref, blk, pair)) for x, (blk, pair) in zip(e, chains)]
    for (blk, pair), x, d in zip(chains, pv, den):
        o = x / d
        o_ref[BLOCK * blk:BLOCK * (blk + 1), 2 * HEAD_DIM * pair:2 * HEAD_DIM * (pair + 1)] = jnp.where(
            low, o[0:BLOCK], o[BLOCK:2 * BLOCK]).astype(o_ref.dtype)


def _attention(sinks, q, kd, vd, bias, batch, seq):
    rows = ATTN_BLOCKS * BLOCK
    nt = seq // rows
    cur = lambda b, n: (b * nt + n, 0)
    prev = lambda b, n: (jnp.maximum((b * nt + n) * ATTN_BLOCKS - 1, 0), 0)
    return pl.pallas_call(
        _attn_kernel,
        grid=(batch, nt),
        in_specs=[
            pl.BlockSpec(memory_space=pltpu.SMEM),
            pl.BlockSpec((rows, _QW), cur),
            pl.BlockSpec((rows, _KW), cur),
            pl.BlockSpec((BLOCK, _KW), prev),
            pl.BlockSpec((rows, _KW), cur),
            pl.BlockSpec((BLOCK, _KW), prev),
            pl.BlockSpec((N_Q_HEADS // 2, 2 * BLOCK, 2 * BLOCK), lambda b, n: (0, 0, 0)),
        ],
        out_specs=pl.BlockSpec((rows, D_ATTN), cur),
        out_shape=jax.ShapeDtypeStruct((batch * seq, D_ATTN), _BF16),
        compiler_params=pltpu.CompilerParams(
            dimension_semantics=("arbitrary", "arbitrary"), vmem_limit_bytes=VMEM_LIMIT),
        name="attention",
    )(sinks, q, kd, kd, vd, vd, bias)


_HEADS_PER_GROUP = GROUP_LANES // HEAD_DIM
_N_GROUPS = D_RWKV // GROUP_LANES
_CHUNKS_PER_PART = 8


def _split3(x):
    hi = x.astype(_BF16)
    r1 = x - hi.astype(_F32)
    mid = r1.astype(_BF16)
    lo = (r1 - mid.astype(_F32)).astype(_BF16)
    return hi, mid, lo


def _interleave(*gens):
    gens = list(gens)
    while gens:
        for g in list(gens):
            try:
                next(g)
            except StopIteration:
                gens.remove(g)


def _rwkv_kernel(p_ref, wl_ref, wg_ref, ones_ref, tri_ref, w0_ref, a0_ref, kk_ref, ka_ref,
                 rk_ref, lng_ref, lnb_ref, o_ref,
                 st_ref, q_s, al_s, kh_s, bh_s, kb_s, bb_s, v_s, plast_s,
                 qh_s, oi_s, lr_s, dcol_s, n_s, g_s, bo_s, o_s):
    tt = p_ref.shape[0]
    n_chunks = tt // CHUNK

    @pl.when(pl.program_id(1) == 0)
    def _():
        st_ref[...] = jnp.zeros_like(st_ref)

    ones_bd = ones_ref[...]
    tri = tri_ref[...]
    lane128 = lax.broadcasted_iota(jnp.int32, (1, LORA_DECAY + LORA_ICLR), 1)
    gl = GROUP_LANES
    lane = lax.broadcasted_iota(jnp.int32, (1, gl), 1)
    head_masks = [(lane >= HEAD_DIM * h) & (lane < HEAD_DIM * (h + 1)) for h in range(_HEADS_PER_GROUP)]
    s_local = lane & (HEAD_DIM - 1)
    trow = lax.broadcasted_iota(jnp.int32, (CHUNK, 1), 0)
    strict = s_local < trow
    incl = s_local <= trow
    eye_tok = jnp.where(s_local == trow, 1.0, 0.0).astype(_F32)
    grow = lax.broadcasted_iota(jnp.int32, (gl, 1), 0)
    same_head = (grow >> 6) == (lane >> 6)

    def stack(x):
        xb = x.astype(_BF16)
        zero = jnp.zeros_like(xb)
        return jnp.concatenate([jnp.where(m, xb, zero) for m in head_masks], axis=0)

    def prepare(part):
        c0 = part * _CHUNKS_PER_PART
        rows = slice(c0 * CHUNK, (c0 + _CHUNKS_PER_PART) * CHUNK)
        r = p_ref[rows, 0:D_RWKV]
        k = p_ref[rows, D_RWKV:2 * D_RWKV]
        v = p_ref[rows, 2 * D_RWKV:3 * D_RWKV]
        z = p_ref[rows, 3 * D_RWKV:3 * D_RWKV + LORA_DECAY + LORA_ICLR]
        zg = p_ref[rows, 3 * D_RWKV + LORA_DECAY + LORA_ICLR:RWKV_COLS]
        zin = jnp.where(lane128 < LORA_DECAY, jnp.tanh(z), z)
        dl = _dot(zin, wl_ref[...])
        g_s[rows, :] = _dot(_sigmoid(zg), wg_ref[...])
        yield
        lw = (-math.exp(-0.5)) * _sigmoid(w0_ref[...] + dl[:, 0:D_RWKV])
        a = _sigmoid(a0_ref[...] + dl[:, D_RWKV:2 * D_RWKV])
        kk = k * kk_ref[...]
        sumsq = _dot(kk * kk, ones_bd)
        yield
        cums, lasts = [], []
        for i in range(_CHUNKS_PER_PART):
            hi, mid, lo = _split3(lw[i * CHUNK:(i + 1) * CHUNK])
            cum_c = _dot(tri, hi) + _dot(tri, mid) + _dot(tri, lo)
            p_last = jnp.exp(cum_c[CHUNK - 1:CHUNK, :])
            plast_s[c0 + i:c0 + i + 1, :] = p_last
            cums.append(cum_c)
            lasts.append(jnp.broadcast_to(p_last, (CHUNK, D_RWKV)))
        cum = jnp.concatenate(cums, axis=0)
        yield
        kk = kk * lax.rsqrt(jnp.maximum(sumsq, 1e-24))
        kp = k * (1.0 + (a - 1.0) * ka_ref[...])
        be = kk * a
        bo_s[rows, :] = _dot(r * kp * rk_ref[...], ones_bd) * v
        v_s[rows, :] = v.astype(_BF16)
        yield
        q_s[rows, :] = (r * jnp.exp(cum)).astype(_BF16)
        al_s[rows, :] = (-kk * jnp.exp(cum - lw)).astype(_BF16)
        yield
        inv = jnp.exp(-cum)
        kh_s[rows, :] = (kp * inv).astype(_BF16)
        bh_s[rows, :] = (be * inv).astype(_BF16)
        yield
        rest = jnp.concatenate(lasts, axis=0) * inv
        kb_s[rows, :] = (kp * rest).astype(_BF16)
        bb_s[rows, :] = (be * rest).astype(_BF16)
        yield

    def intra(part):
        chains = [(part * _CHUNKS_PER_PART + u, gi) for u in range(_CHUNKS_PER_PART) for gi in range(_N_GROUPS)]
        idx = [(slice(c * CHUNK, (c + 1) * CHUNK), slice(gl * gi, gl * (gi + 1))) for c, gi in chains]
        al = [al_s[r, s] for r, s in idx]
        q = [q_s[r, s] for r, s in idx]
        v = [v_s[r, s] for r, s in idx]
        gram = [_dot_nt(jnp.concatenate([a, b], axis=0),
                        jnp.concatenate([stack(kh_s[r, s]), stack(bh_s[r, s])], axis=0))
                for a, b, (r, s) in zip(al, q, idx)]
        yield
        a_ak = [jnp.where(strict, g[0:CHUNK, 0:gl], 0.0) for g in gram]
        a_ab = [jnp.where(strict, g[0:CHUNK, gl:2 * gl], 0.0) for g in gram]
        a_qk = [jnp.where(incl, g[CHUNK:2 * CHUNK, 0:gl], 0.0) for g in gram]
        a_qb = [jnp.where(incl, g[CHUNK:2 * CHUNK, gl:2 * gl], 0.0) for g in gram]
        akv = [_dot(a, stack(b)) for a, b in zip(a_ak, v)]
        pw = [_dot(n, stack(n)) for n in a_ab]
        tinv = [eye_tok + n for n in a_ab]
        yield
        for level in range(1, 6):
            rhs = [stack(x) for x in pw]
            if level < 5:
                res = [_dot(jnp.concatenate([x, t], axis=0), w) for x, t, w in zip(pw, tinv, rhs)]
                pw = [x[0:CHUNK] for x in res]
                tinv = [t + x[CHUNK:2 * CHUNK] for t, x in zip(tinv, res)]
            else:
                tinv = [t + _dot(t, w) for t, w in zip(tinv, rhs)]
            yield
        y = [_dot(t, jnp.concatenate([stack(a), stack(b)], axis=1)) for t, a, b in zip(tinv, al, akv)]
        al_hat = [x[:, 0:gl] for x in y]
        u_v = [x[:, gl:2 * gl] for x in y]
        yield
        qh = [_dot(a, stack(b)) for a, b in zip(a_qb, al_hat)]
        oi = [_dot(jnp.concatenate([a, b], axis=1), jnp.concatenate([stack(c_), stack(d)], axis=0))
              for a, b, c_, d in zip(a_qk, a_qb, v, u_v)]
        yield
        tr = [jnp.concatenate([kb_s[r, s].astype(_F32), bb_s[r, s].astype(_F32),
                               jnp.broadcast_to(plast_s[c:c + 1, s], (2 * CHUNK, gl))], axis=0).T
              for (c, gi), (r, s) in zip(chains, idx)]
        upd = [_dot(t[:, 0:2 * CHUNK],
                    jnp.concatenate([jnp.concatenate([c_, jnp.zeros_like(c_)], axis=1),
                                     jnp.concatenate([d, e], axis=1).astype(_BF16)], axis=0))
               for t, c_, d, e in zip(tr, v, u_v, al_hat)]
        yield
        for (c, gi), (r, s), q_i, qh_i, oi_i, upd_i, t in zip(chains, idx, q, qh, oi, upd, tr):
            qh_s[r, s] = (q_i.astype(_F32) + qh_i).astype(_BF16)
            oi_s[r, s] = oi_i
            n_s[c, gi] = jnp.where(same_head, upd_i[:, 0:gl], 0.0)
            lr_s[c, gi] = jnp.where(same_head, upd_i[:, gl:2 * gl], 0.0).astype(_BF16)
            dcol_s[c, gi] = t[:, 2 * CHUNK:4 * CHUNK]
        yield

    n_parts = n_chunks // _CHUNKS_PER_PART
    _interleave(prepare(0))
    for part in range(n_parts):
        _interleave(intra(part), *([prepare(part + 1)] if part + 1 < n_parts else []))

    def state_body(c, carry):
        rows = pl.ds(pl.multiple_of(c * CHUNK, CHUNK), CHUNK)
        sls = [slice(gl * gi, gl * (gi + 1)) for gi in range(_N_GROUPS)]
        st = [st_ref[gi] for gi in range(_N_GROUPS)]
        sb = [x.astype(_BF16) for x in st]
        new = [_dot(lr_s[c, gi], sb[gi]) for gi in range(_N_GROUPS)]
        out = [_dot(qh_s[rows, sls[gi]], sb[gi]) for gi in range(_N_GROUPS)]
        for gi in range(_N_GROUPS):
            dcol = dcol_s[c, gi]
            st_ref[gi] = jnp.concatenate([dcol] * (gl // (2 * CHUNK)), axis=1) * st[gi] + new[gi] + n_s[c, gi]
            o_s[rows, sls[gi]] = out[gi] + oi_s[rows, sls[gi]]
        return carry

    lax.fori_loop(0, n_chunks, state_body, 0)

    o = o_s[...]
    mu = _dot(o, ones_bd) * (1.0 / HEAD_DIM)
    d = o - mu
    var = _dot(d * d, ones_bd) * (1.0 / HEAD_DIM)
    on = d * lax.rsqrt(var + GN_EPS) * lng_ref[...] + lnb_ref[...]
    o_ref[...] = ((on + bo_s[...]) * g_s[...]).astype(o_ref.dtype)


def _rwkv(p, wl, wg, ones_bd, tri, w0, a0, k_k, k_a, r_k, ln_g, ln_b, batch, seq):
    tt = RWKV_ROWS
    nt = seq // tt
    vec = lambda n: pl.BlockSpec((1, n), lambda b, t: (0, 0))
    full = lambda a: pl.BlockSpec(a.shape, lambda b, t: (0,) * a.ndim)
    tile = lambda dt: pltpu.VMEM((tt, D_RWKV), dt)
    per_chunk = lambda dt: pltpu.VMEM((tt // CHUNK, _N_GROUPS, GROUP_LANES, GROUP_LANES), dt)
    return pl.pallas_call(
        _rwkv_kernel,
        grid=(batch, nt),
        in_specs=[
            pl.BlockSpec((tt, RWKV_COLS), lambda b, t: (b * nt + t, 0)),
            full(wl), full(wg), full(ones_bd), full(tri),
            vec(D_RWKV), vec(D_RWKV), vec(D_RWKV), vec(D_RWKV), vec(D_RWKV), vec(D_RWKV), vec(D_RWKV),
        ],
        out_specs=pl.BlockSpec((tt, D_RWKV), lambda b, t: (b * nt + t, 0)),
        out_shape=jax.ShapeDtypeStruct((batch * seq, D_RWKV), _BF16),
        scratch_shapes=[
            pltpu.VMEM((_N_GROUPS, GROUP_LANES, GROUP_LANES), _F32),
            tile(_BF16), tile(_BF16), tile(_BF16), tile(_BF16), tile(_BF16), tile(_BF16), tile(_BF16),
            pltpu.VMEM((tt // CHUNK, D_RWKV), _F32),
            tile(_BF16), tile(_F32), per_chunk(_BF16),
            pltpu.VMEM((tt // CHUNK, _N_GROUPS, GROUP_LANES, 2 * CHUNK), _F32), per_chunk(_F32),
            tile(_F32), tile(_F32), tile(_F32),
        ],
        compiler_params=pltpu.CompilerParams(
            dimension_semantics=("arbitrary", "arbitrary"), vmem_limit_bytes=VMEM_LIMIT),
        name="rwkv",
    )(p, wl, wg, ones_bd, tri, w0, a0, k_k, k_a, r_k, ln_g, ln_b)


_N_FF_STEPS = D_FF // FFN_COLS
_PIECE_ROWS = 64
_PIECE_LANES = 128


_GELU_C1 = 0.7978845608028654
_GELU_C2 = _GELU_C1 * 0.044715


def _two_gelu_tanh(x):
    return x + x * jnp.tanh(x * (_GELU_C1 + _GELU_C2 * (x * x)))


def _out_ffn_kernel(attn_ref, rw_ref, x_ref, wo_ref, gmp_ref, gfp_ref, gfo_ref,
                    wup_ref, cw_ref, cb_ref, wdn_ref, out_ref,
                    carry_ref, h_ref, x1_ref, u_s, act_s):
    tm = x_ref.shape[0]

    @pl.when(pl.program_id(1) == 0)
    def _():
        carry_ref[...] = jnp.zeros_like(carry_ref)

    mix = _dot(attn_ref[...], wo_ref[0:D_ATTN, :]) + _dot(rw_ref[...], wo_ref[D_ATTN:D_MODEL, :])
    x1 = x_ref[...] + _rms_norm(mix, gmp_ref[...])
    x1_ref[...] = x1
    h_ref[...] = _rms_norm(x1, gfp_ref[...]).astype(_BF16)

    def up(c):
        slot = c % 2
        h = h_ref[...]
        for half, col0 in enumerate((c * FFN_COLS, D_FF + c * FFN_COLS)):
            u = jnp.dot(h, wup_ref[:, col0:col0 + FFN_COLS], preferred_element_type=_F32)
            lanes = slice(half * FFN_COLS, (half + 1) * FFN_COLS)
            u_s[slot, 8:tm + 8, lanes] = u
            u_s[slot, 0:8, lanes] = carry_ref[:, col0:col0 + FFN_COLS]
            carry_ref[:, col0:col0 + FFN_COLS] = u[tm - 8:tm]

    def conv_act(c):
        slot = c % 2

        def conv(r0, col, wcol):
            w = u_s[slot, r0:r0 + _PIECE_ROWS + 8, col:col + _PIECE_LANES]
            lanes = slice(wcol, wcol + _PIECE_LANES)
            r1 = pltpu.roll(w, 1, 0)[8:]
            r2 = pltpu.roll(w, 2, 0)[8:]
            return cb_ref[:, lanes] + cw_ref[0:1, lanes] * r2 + cw_ref[1:2, lanes] * r1 + cw_ref[2:3, lanes] * w[8:]

        for r0 in range(0, tm, _PIECE_ROWS):
            for col in range(0, FFN_COLS, _PIECE_LANES):
                gate = conv(r0, col, c * FFN_COLS + col)
                half_val = conv(r0, FFN_COLS + col, D_FF + c * FFN_COLS + col)
                act_s[slot, r0:r0 + _PIECE_ROWS, col:col + _PIECE_LANES] = (
                    _two_gelu_tanh(gate) * half_val).astype(_BF16)

    up(0)
    for c in range(_N_FF_STEPS):
        if c + 1 < _N_FF_STEPS:
            up(c + 1)
        conv_act(c)
        d = jnp.dot(act_s[c % 2], wdn_ref[c], preferred_element_type=_F32)
        if c == 0:
            out_ref[...] = d
        else:
            out_ref[...] += d
    out_ref[...] = x1_ref[...] + _rms_norm(out_ref[...], gfo_ref[...])


def _out_ffn(attn, rw, x2, wo, g_mix_post, g_ffn_pre, g_ffn_post, wup, cw, cb, wdn, batch, seq):
    tm = FFN_ROWS
    nt = seq // tm
    rows = lambda n: pl.BlockSpec((tm, n), lambda b, t: (b * nt + t, 0))
    vec = pl.BlockSpec((1, D_MODEL), lambda b, t: (0, 0))
    const = lambda a: pl.BlockSpec(a.shape, lambda b, t: (0,) * a.ndim, pipeline_mode=pl.Buffered(1))
    return pl.pallas_call(
        _out_ffn_kernel,
        grid=(batch, nt),
        in_specs=[rows(D_ATTN), rows(D_RWKV), rows(D_MODEL), const(wo), vec, vec, vec,
                  const(wup), const(cw), const(cb), const(wdn)],
        out_specs=rows(D_MODEL),
        out_shape=jax.ShapeDtypeStruct((batch * seq, D_MODEL), _F32),
        scratch_shapes=[
            pltpu.VMEM((8, 2 * D_FF), _F32),
            pltpu.VMEM((tm, D_MODEL), _BF16),
            pltpu.VMEM((tm, D_MODEL), _F32),
            pltpu.VMEM((2, tm + 8, 2 * FFN_COLS), _F32),
            pltpu.VMEM((2, tm, FFN_COLS), _BF16),
        ],
        compiler_params=pltpu.CompilerParams(
            dimension_semantics=("arbitrary", "arbitrary"), vmem_limit_bytes=VMEM_LIMIT),
        name="out_ffn",
    )(attn, rw, x2, wo, g_mix_post, g_ffn_pre, g_ffn_post, wup, cw, cb, wdn)


def _t5_bucket_table():
    rel = (np.arange(BLOCK)[:, None] + BLOCK) - np.arange(2 * BLOCK)[None, :]
    n = np.maximum(rel, 0)
    max_exact = N_BUCKETS // 2
    large = max_exact + (np.log(np.maximum(n, 1).astype(np.float32) / np.float32(max_exact))
                         / np.float32(math.log(MAX_DISTANCE / max_exact))
                         * np.float32(N_BUCKETS - max_exact)).astype(np.int32)
    large = np.minimum(large, N_BUCKETS - 1)
    return np.where(n < max_exact, n, large).astype(np.int32)


def _row(a):
    return a.reshape(1, -1)


def _in_proj_stage(x2, norm_mix_pre, w_in, shift_mix, seq):
    return _in_proj(x2, _row(norm_mix_pre), w_in.astype(_BF16), _row(shift_mix), seq)


def _rwkv_stage(p, w0, w_decay_up, a0, w_iclr_up, w_gate_up, k_k, k_a, r_k, ln_x_g, ln_x_b, batch, seq):
    zeros = jnp.zeros((LORA_DECAY, D_RWKV), _F32)
    w_lora = jnp.concatenate([jnp.concatenate([w_decay_up, zeros], axis=1),
                              jnp.concatenate([zeros, w_iclr_up], axis=1)], axis=0).astype(_BF16)
    hid = jnp.arange(D_RWKV) // HEAD_DIM
    ones_bd = (hid[:, None] == hid[None, :]).astype(_BF16)
    tri = (jnp.arange(CHUNK)[:, None] >= jnp.arange(CHUNK)[None, :]).astype(_BF16)
    return _rwkv(p, w_lora, w_gate_up.astype(_BF16), ones_bd, tri, _row(w0), _row(a0), _row(k_k),
                 _row(k_a), _row(r_k), _row(ln_x_g), _row(ln_x_b), batch, seq)


def _ffn_stage(attn, rw, x2, norm_mix_post, norm_ffn_pre, norm_ffn_post, w_out, w_ffn_up, conv_w, conv_b,
               w_ffn_down, batch, seq):
    half_val = jnp.concatenate([jnp.ones((D_FF,), _F32), jnp.full((D_FF,), 0.5, _F32)])
    cw = jnp.pad(conv_w * half_val, ((0, 8 - conv_w.shape[0]), (0, 0)))
    cb = _row(conv_b * half_val)
    wdn = w_ffn_down.reshape(_N_FF_STEPS, FFN_COLS, D_MODEL).astype(_BF16)
    return _out_ffn(attn, rw, x2, w_out.astype(_BF16), _row(norm_mix_post), _row(norm_ffn_pre),
                    _row(norm_ffn_post), w_ffn_up.astype(_BF16), cw, cb, wdn, batch, seq)


def _layer(x2, batch, seq, norm_mix_pre, norm_mix_post, norm_ffn_pre, norm_ffn_post, w_in, bias, sinks,
           shift_mix, w0, w_decay_up, a0, w_iclr_up, w_gate_up, k_k, k_a, r_k, ln_x_g, ln_x_b,
           w_out, w_ffn_up, conv_w, conv_b, w_ffn_down):
    q, kd, vd, p = _in_proj_stage(x2, norm_mix_pre, w_in, shift_mix, seq)
    attn = _attention(sinks, q, kd, vd, bias, batch, seq)
    rw = _rwkv_stage(p, w0, w_decay_up, a0, w_iclr_up, w_gate_up, k_k, k_a, r_k, ln_x_g, ln_x_b,
                     batch, seq)
    return _ffn_stage(attn, rw, x2, norm_mix_post, norm_ffn_pre, norm_ffn_post, w_out, w_ffn_up, conv_w, conv_b,
                      w_ffn_down, batch, seq)


def kernel(x, norm_mix_pre, norm_mix_post, norm_ffn_pre, norm_ffn_post, w_in, rel_bias, sinks, rwkv_shift_mix, w0, w_decay_up, a0, w_iclr_up, w_gate_up, k_k, k_a, r_k, ln_x_g, ln_x_b, w_out, w_ffn_up, conv_w, conv_b, w_ffn_down):
    batch, seq, _ = x.shape
    depth = w_in.shape[0]
    bias = _bias_table(rel_bias.astype(_F32), jnp.asarray(_t5_bucket_table()))
    bias = bias.reshape(N_Q_HEADS // 2, 2 * BLOCK, 2 * BLOCK)
    x2 = x.reshape(batch * seq, D_MODEL)
    for l in range(depth):
        x2 = _layer(x2, batch, seq, norm_mix_pre[l], norm_mix_post[l], norm_ffn_pre[l], norm_ffn_post[l], w_in[l],
                    bias, sinks[l], rwkv_shift_mix[l], w0[l], w_decay_up[l], a0[l], w_iclr_up[l], w_gate_up[l],
                    k_k[l], k_a[l], r_k[l].reshape(-1), ln_x_g[l], ln_x_b[l], w_out[l], w_ffn_up[l], conv_w[l],
                    conv_b[l], w_ffn_down[l])
    return x2.reshape(batch, seq, D_MODEL)
```

```python
import functools
import math

import jax
import jax.numpy as jnp
import numpy as np
from jax import lax
from jax.experimental import pallas as pl
from jax.experimental.pallas import tpu as pltpu

D_MODEL = 1024
HEAD_DIM = 64
D_ATTN = 512
D_RWKV = 512
N_Q_HEADS = 8
N_KV_HEADS = 2
D_KV = 128
WINDOW = 128
BLOCK = 128
N_BUCKETS = 32
MAX_DISTANCE = 128
LORA_DECAY = 64
LORA_ICLR = 64
LORA_GATE = 128
RWKV_COLS = 3 * D_RWKV + LORA_DECAY + LORA_ICLR + LORA_GATE
D_FF = 4 * D_MODEL
NORM_EPS = 1e-6
GN_EPS = 64e-5
NEG_INF = -1e30

IN_PROJ_ROWS = 1024
ATTN_BLOCKS = 4
RWKV_ROWS = 512
CHUNK = 64
GROUP_LANES = 128
FFN_ROWS = 512
FFN_COLS = 512
VMEM_LIMIT = 56 * 1024 * 1024

_BF16 = jnp.bfloat16
_F32 = jnp.float32


def _dot(a, b):
    return jnp.dot(a.astype(_BF16), b.astype(_BF16), preferred_element_type=_F32)


def _dot_nt(a, b):
    return lax.dot_general(a.astype(_BF16), b.astype(_BF16), (((1,), (1,)), ((), ())),
                           preferred_element_type=_F32)


def _rms_norm(x, g):
    return x * lax.rsqrt(jnp.mean(x * x, axis=-1, keepdims=True) + NORM_EPS) * g


def _sigmoid(x):
    return 0.5 + 0.5 * jnp.tanh(0.5 * x)


def _bias_kernel(rb_ref, bucket_ref, out_ref):
    h = pl.program_id(0)
    bucket = bucket_ref[...]
    qi = lax.broadcasted_iota(jnp.int32, (BLOCK, 2 * BLOCK), 0)
    kj = lax.broadcasted_iota(jnp.int32, (BLOCK, 2 * BLOCK), 1)
    rel = qi + BLOCK - kj
    acc = jnp.zeros((BLOCK, 2 * BLOCK), _F32)
    for b in range(N_BUCKETS):
        acc = jnp.where(bucket == b, rb_ref[b, h], acc)
    out_ref[0] = jnp.where((rel >= 0) & (rel < WINDOW), acc, NEG_INF)


def _bias_table(rel_bias, bucket):
    return pl.pallas_call(
        _bias_kernel,
        grid=(N_Q_HEADS,),
        in_specs=[
            pl.BlockSpec(memory_space=pltpu.SMEM),
            pl.BlockSpec((BLOCK, 2 * BLOCK), lambda h: (0, 0)),
        ],
        out_specs=pl.BlockSpec((1, BLOCK, 2 * BLOCK), lambda h: (h, 0, 0)),
        out_shape=jax.ShapeDtypeStruct((N_Q_HEADS, BLOCK, 2 * BLOCK), _F32),
        name="bias_table",
    )(rel_bias, bucket)


_QW = D_ATTN
_KW = 2 * D_KV
_IN_COLS = D_ATTN + 2 * D_KV + RWKV_COLS


def _dup_heads(t):
    low = lax.broadcasted_iota(jnp.int32, (1, D_KV), 1) < HEAD_DIM
    swapped = pltpu.roll(t, HEAD_DIM, 1)
    return jnp.concatenate([jnp.where(low, t, swapped), jnp.where(low, swapped, t)], axis=1)


def _in_proj_kernel(tiles_per_seq, x_ref, g_ref, w_ref, mix_ref, q_ref, k_ref, v_ref, p_ref, carry_ref):
    @pl.when(pl.program_id(0) % tiles_per_seq == 0)
    def _():
        carry_ref[...] = jnp.zeros_like(carry_ref)

    h = _rms_norm(x_ref[...], g_ref[...]).astype(_BF16)
    q_ref[...] = (_dot(h, w_ref[:, 0:D_ATTN]) * (HEAD_DIM ** -0.5)).astype(_BF16)
    k_ref[...] = _dup_heads(_dot(h, w_ref[:, D_ATTN:D_ATTN + D_KV])).astype(_BF16)
    v_ref[...] = _dup_heads(_dot(h, w_ref[:, D_ATTN + D_KV:D_ATTN + 2 * D_KV])).astype(_BF16)
    p = _dot(h, w_ref[:, D_ATTN + 2 * D_KV:_IN_COLS])
    rolled = pltpu.roll(p, 1, 0)
    row8 = lax.broadcasted_iota(jnp.int32, (8, 1), 0)
    head = jnp.where(row8 == 0, carry_ref[...], rolled[0:8])
    carry_ref[...] = rolled[0:8]
    p_prev = jnp.concatenate([head, rolled[8:]], axis=0)
    p_ref[...] = p + (p_prev - p) * mix_ref[...]


def _in_proj(x2, g, w, mix, seq):
    t = x2.shape[0]
    tm = IN_PROJ_ROWS
    return pl.pallas_call(
        functools.partial(_in_proj_kernel, seq // tm),
        grid=(t // tm,),
        in_specs=[
            pl.BlockSpec((tm, D_MODEL), lambda i: (i, 0)),
            pl.BlockSpec((1, D_MODEL), lambda i: (0, 0)),
            pl.BlockSpec((D_MODEL, _IN_COLS), lambda i: (0, 0)),
            pl.BlockSpec((1, RWKV_COLS), lambda i: (0, 0)),
        ],
        scratch_shapes=[pltpu.VMEM((8, RWKV_COLS), _F32)],
        out_specs=[
            pl.BlockSpec((tm, _QW), lambda i: (i, 0)),
            pl.BlockSpec((tm, _KW), lambda i: (i, 0)),
            pl.BlockSpec((tm, _KW), lambda i: (i, 0)),
            pl.BlockSpec((tm, RWKV_COLS), lambda i: (i, 0)),
        ],
        out_shape=[
            jax.ShapeDtypeStruct((t, _QW), _BF16),
            jax.ShapeDtypeStruct((t, _KW), _BF16),
            jax.ShapeDtypeStruct((t, _KW), _BF16),
            jax.ShapeDtypeStruct((t, RWKV_COLS), _F32),
        ],
        compiler_params=pltpu.CompilerParams(
            dimension_semantics=("arbitrary",), vmem_limit_bytes=VMEM_LIMIT),
        name="in_proj",
    )(x2, g, w, mix)


def _attn_kernel(sink_ref, q_ref, kc_ref, kp_ref, vc_ref, vp_ref, bias_ref, o_ref):
    step = pl.program_id(1)
    col = lax.broadcasted_iota(jnp.int32, (1, 2 * BLOCK), 1)
    first_valid = col >= jnp.where(step == 0, BLOCK, 0)
    lane = lax.broadcasted_iota(jnp.int32, (1, 2 * HEAD_DIM), 1)
    low = lane < HEAD_DIM
    top = lax.broadcasted_iota(jnp.int32, (2 * BLOCK, 1), 0) < BLOCK
    chains = [(blk, pair) for blk in range(ATTN_BLOCKS) for pair in range(N_Q_HEADS // 2)]

    def band(cur_ref, prev_ref, blk, pair):
        j = (2 * pair) // (N_Q_HEADS // N_KV_HEADS)
        kv_lanes = slice(2 * HEAD_DIM * j, 2 * HEAD_DIM * (j + 1))
        prev = prev_ref[:, kv_lanes] if blk == 0 else cur_ref[BLOCK * (blk - 1):BLOCK * blk, kv_lanes]
        return jnp.concatenate([prev, cur_ref[BLOCK * blk:BLOCK * (blk + 1), kv_lanes]], axis=0)

    s = []
    for blk, pair in chains:
        qp = q_ref[BLOCK * blk:BLOCK * (blk + 1), 2 * HEAD_DIM * pair:2 * HEAD_DIM * (pair + 1)]
        zero = jnp.zeros_like(qp)
        q_st = jnp.concatenate([jnp.where(low, qp, zero), jnp.where(low, zero, qp)], axis=0)
        sc = _dot_nt(q_st, band(kc_ref, kp_ref, blk, pair)) + bias_ref[pair]
        s.append(jnp.where(first_valid, sc, NEG_INF) if blk == 0 else sc)
    sink = [jnp.where(top, sink_ref[2 * pair], sink_ref[2 * pair + 1]) for _, pair in chains]
    m = [jnp.maximum(jnp.max(x, axis=-1, keepdims=True), k) for x, k in zip(s, sink)]
    e = [jnp.exp(x - mm) for x, mm in zip(s, m)]
    den = [jnp.sum(x, axis=-1, keepdims=True) + jnp.exp(k - mm) for x, k, mm in zip(e, sink, m)]
    pv = [_dot(x, band(vc_ref, vp_ref, blk, pair)) for x, (blk, pair) in zip(e, chains)]
    for (blk, pair), x, d in zip(chains, pv, den):
        o = x / d
        o_ref[BLOCK * blk:BLOCK * (blk + 1), 2 * HEAD_DIM * pair:2 * HEAD_DIM * (pair + 1)] = jnp.where(
            low, o[0:BLOCK], o[BLOCK:2 * BLOCK]).astype(o_ref.dtype)


def _attention(sinks, q, kd, vd, bias, batch, seq):
    rows = ATTN_BLOCKS * BLOCK
    nt = seq // rows
    cur = lambda b, n: (b * nt + n, 0)
    prev = lambda b, n: (jnp.maximum((b * nt + n) * ATTN_BLOCKS - 1, 0), 0)
    return pl.pallas_call(
        _attn_kernel,
        grid=(batch, nt),
        in_specs=[
            pl.BlockSpec(memory_space=pltpu.SMEM),
            pl.BlockSpec((rows, _QW), cur),
            pl.BlockSpec((rows, _KW), cur),
            pl.BlockSpec((BLOCK, _KW), prev),
            pl.BlockSpec((rows, _KW), cur),
            pl.BlockSpec((BLOCK, _KW), prev),
            pl.BlockSpec((N_Q_HEADS // 2, 2 * BLOCK, 2 * BLOCK), lambda b, n: (0, 0, 0)),
        ],
        out_specs=pl.BlockSpec((rows, D_ATTN), cur),
        out_shape=jax.ShapeDtypeStruct((batch * seq, D_ATTN), _BF16),
        compiler_params=pltpu.CompilerParams(
            dimension_semantics=("arbitrary", "arbitrary"), vmem_limit_bytes=VMEM_LIMIT),
        name="attention",
    )(sinks, q, kd, kd, vd, vd, bias)


_HEADS_PER_GROUP = GROUP_LANES // HEAD_DIM
_N_GROUPS = D_RWKV // GROUP_LANES
_CHUNKS_PER_PART = 4


def _split3(x):
    hi = x.astype(_BF16)
    r1 = x - hi.astype(_F32)
    mid = r1.astype(_BF16)
    lo = (r1 - mid.astype(_F32)).astype(_BF16)
    return hi, mid, lo


def _interleave(*gens):
    gens = list(gens)
    while gens:
        for g in list(gens):
            try:
                next(g)
            except StopIteration:
                gens.remove(g)


def _rwkv_kernel(p_ref, wl_ref, wg_ref, ones_ref, tri_ref, w0_ref, a0_ref, kk_ref, ka_ref,
                 rk_ref, lng_ref, lnb_ref, o_ref,
                 st_ref, q_s, al_s, kh_s, bh_s, kb_s, bb_s, v_s, plast_s,
                 qh_s, oi_s, lr_s, dcol_s, n_s, g_s, bo_s, o_s):
    tt = p_ref.shape[0]
    n_chunks = tt // CHUNK

    @pl.when(pl.program_id(1) == 0)
    def _():
        st_ref[...] = jnp.zeros_like(st_ref)

    ones_bd = ones_ref[...]
    tri = tri_ref[...]
    lane128 = lax.broadcasted_iota(jnp.int32, (1, LORA_DECAY + LORA_ICLR), 1)
    gl = GROUP_LANES
    lane = lax.broadcasted_iota(jnp.int32, (1, gl), 1)
    head_masks = [(lane >= HEAD_DIM * h) & (lane < HEAD_DIM * (h + 1)) for h in range(_HEADS_PER_GROUP)]
    s_local = lane & (HEAD_DIM - 1)
    trow = lax.broadcasted_iota(jnp.int32, (CHUNK, 1), 0)
    strict = s_local < trow
    incl = s_local <= trow
    eye_tok = jnp.where(s_local == trow, 1.0, 0.0).astype(_F32)
    grow = lax.broadcasted_iota(jnp.int32, (gl, 1), 0)
    same_head = (grow >> 6) == (lane >> 6)

    def stack(x):
        xb = x.astype(_BF16)
        zero = jnp.zeros_like(xb)
        return jnp.concatenate([jnp.where(m, xb, zero) for m in head_masks], axis=0)

    def prepare(part):
        c0 = part * _CHUNKS_PER_PART
        rows = slice(c0 * CHUNK, (c0 + _CHUNKS_PER_PART) * CHUNK)
        r = p_ref[rows, 0:D_RWKV]
        k = p_ref[rows, D_RWKV:2 * D_RWKV]
        v = p_ref[rows, 2 * D_RWKV:3 * D_RWKV]
        z = p_ref[rows, 3 * D_RWKV:3 * D_RWKV + LORA_DECAY + LORA_ICLR]
        zg = p_ref[rows, 3 * D_RWKV + LORA_DECAY + LORA_ICLR:RWKV_COLS]
        zin = jnp.where(lane128 < LORA_DECAY, jnp.tanh(z), z)
        dl = _dot(zin, wl_ref[...])
        g_s[rows, :] = _dot(_sigmoid(zg), wg_ref[...])
        yield
        lw = (-math.exp(-0.5)) * _sigmoid(w0_ref[...] + dl[:, 0:D_RWKV])
        a = _sigmoid(a0_ref[...] + dl[:, D_RWKV:2 * D_RWKV])
        kk = k * kk_ref[...]
        sumsq = _dot(kk * kk, ones_bd)
        yield
        cums, lasts = [], []
        for i in range(_CHUNKS_PER_PART):
            hi, mid, lo = _split3(lw[i * CHUNK:(i + 1) * CHUNK])
            cum_c = _dot(tri, hi) + _dot(tri, mid) + _dot(tri, lo)
            p_last = jnp.exp(cum_c[CHUNK - 1:CHUNK, :])
            plast_s[c0 + i:c0 + i + 1, :] = p_last
            cums.append(cum_c)
            lasts.append(jnp.broadcast_to(p_last, (CHUNK, D_RWKV)))
        cum = jnp.concatenate(cums, axis=0)
        yield
        kk = kk * lax.rsqrt(jnp.maximum(sumsq, 1e-24))
        kp = k * (1.0 + (a - 1.0) * ka_ref[...])
        be = kk * a
        bo_s[rows, :] = _dot(r * kp * rk_ref[...], ones_bd) * v
        v_s[rows, :] = v.astype(_BF16)
        yield
        q_s[rows, :] = (r * jnp.exp(cum)).astype(_BF16)
        al_s[rows, :] = (-kk * jnp.exp(cum - lw)).astype(_BF16)
        yield
        inv = jnp.exp(-cum)
        kh_s[rows, :] = (kp * inv).astype(_BF16)
        bh_s[rows, :] = (be * inv).astype(_BF16)
        yield
        rest = jnp.concatenate(lasts, axis=0) * inv
        kb_s[rows, :] = (kp * rest).astype(_BF16)
        bb_s[rows, :] = (be * rest).astype(_BF16)
        yield

    def intra(part):
        chains = [(part * _CHUNKS_PER_PART + u, gi) for u in range(_CHUNKS_PER_PART) for gi in range(_N_GROUPS)]
        idx = [(slice(c * CHUNK, (c + 1) * CHUNK), slice(gl * gi, gl * (gi + 1))) for c, gi in chains]
        al = [al_s[r, s] for r, s in idx]
        q = [q_s[r, s] for r, s in idx]
        v = [v_s[r, s] for r, s in idx]
        gram = [_dot_nt(jnp.concatenate([a, b], axis=0),
                        jnp.concatenate([stack(kh_s[r, s]), stack(bh_s[r, s])], axis=0))
                for a, b, (r, s) in zip(al, q, idx)]
        yield
        a_ak = [jnp.where(strict, g[0:CHUNK, 0:gl], 0.0) for g in gram]
        a_ab = [jnp.where(strict, g[0:CHUNK, gl:2 * gl], 0.0) for g in gram]
        a_qk = [jnp.where(incl, g[CHUNK:2 * CHUNK, 0:gl], 0.0) for g in gram]
        a_qb = [jnp.where(incl, g[CHUNK:2 * CHUNK, gl:2 * gl], 0.0) for g in gram]
        akv = [_dot(a, stack(b)) for a, b in zip(a_ak, v)]
        pw = [_dot(n, stack(n)) for n in a_ab]
        tinv = [eye_tok + n for n in a_ab]
        yield
        for level in range(1, 6):
            rhs = [stack(x) for x in pw]
            if level < 5:
                res = [_dot(jnp.concatenate([x, t], axis=0), w) for x, t, w in zip(pw, tinv, rhs)]
                pw = [x[0:CHUNK] for x in res]
                tinv = [t + x[CHUNK:2 * CHUNK] for t, x in zip(tinv, res)]
            else:
                tinv = [t + _dot(t, w) for t, w in zip(tinv, rhs)]
            yield
        y = [_dot(t, jnp.concatenate([stack(a), stack(b)], axis=1)) for t, a, b in zip(tinv, al, akv)]
        al_hat = [x[:, 0:gl] for x in y]
        u_v = [x[:, gl:2 * gl] for x in y]
        yield
        qh = [_dot(a, stack(b)) for a, b in zip(a_qb, al_hat)]
        oi = [_dot(jnp.concatenate([a, b], axis=1), jnp.concatenate([stack(c_), stack(d)], axis=0))
              for a, b, c_, d in zip(a_qk, a_qb, v, u_v)]
        yield
        tr = [jnp.concatenate([kb_s[r, s].astype(_F32), bb_s[r, s].astype(_F32),
                               jnp.broadcast_to(plast_s[c:c + 1, s], (2 * CHUNK, gl))], axis=0).T
              for (c, gi), (r, s) in zip(chains, idx)]
        upd = [_dot(t[:, 0:2 * CHUNK],
                    jnp.concatenate([jnp.concatenate([c_, jnp.zeros_like(c_)], axis=1),
                                     jnp.concatenate([d, e], axis=1).astype(_BF16)], axis=0))
               for t, c_, d, e in zip(tr, v, u_v, al_hat)]
        yield
        for (c, gi), (r, s), q_i, qh_i, oi_i, upd_i, t in zip(chains, idx, q, qh, oi, upd, tr):
            qh_s[r, s] = (q_i.astype(_F32) + qh_i).astype(_BF16)
            oi_s[r, s] = oi_i
            n_s[c, gi] = jnp.where(same_head, upd_i[:, 0:gl], 0.0)
            lr_s[c, gi] = jnp.where(same_head, upd_i[:, gl:2 * gl], 0.0).astype(_BF16)
            dcol_s[c, gi] = t[:, 2 * CHUNK:4 * CHUNK]
        yield

    def sequential(part):
        sls = [slice(gl * gi, gl * (gi + 1)) for gi in range(_N_GROUPS)]
        for c in range(part * _CHUNKS_PER_PART, (part + 1) * _CHUNKS_PER_PART):
            rows = slice(c * CHUNK, (c + 1) * CHUNK)
            st = [st_ref[gi] for gi in range(_N_GROUPS)]
            sb = [x.astype(_BF16) for x in st]
            new = [_dot(lr_s[c, gi], sb[gi]) for gi in range(_N_GROUPS)]
            out = [_dot(qh_s[rows, sls[gi]], sb[gi]) for gi in range(_N_GROUPS)]
            yield
            for gi in range(_N_GROUPS):
                dcol = dcol_s[c, gi]
                st_ref[gi] = (jnp.concatenate([dcol] * (gl // (2 * CHUNK)), axis=1) * st[gi] + new[gi]
                              + n_s[c, gi])
                o_s[rows, sls[gi]] = out[gi] + oi_s[rows, sls[gi]]
            yield

    n_parts = n_chunks // _CHUNKS_PER_PART
    _interleave(prepare(0))
    for part in range(n_parts):
        others = ([prepare(part + 1)] if part + 1 < n_parts else []) + ([sequential(part - 1)] if part else [])
        _interleave(intra(part), *others)
    _interleave(sequential(n_parts - 1))

    o = o_s[...]
    mu = _dot(o, ones_bd) * (1.0 / HEAD_DIM)
    d = o - mu
    var = _dot(d * d, ones_bd) * (1.0 / HEAD_DIM)
    on = d * lax.rsqrt(var + GN_EPS) * lng_ref[...] + lnb_ref[...]
    o_ref[...] = ((on + bo_s[...]) * g_s[...]).astype(o_ref.dtype)


def _rwkv(p, wl, wg, ones_bd, tri, w0, a0, k_k, k_a, r_k, ln_g, ln_b, batch, seq):
    tt = RWKV_ROWS
    nt = seq // tt
    vec = lambda n: pl.BlockSpec((1, n), lambda b, t: (0, 0))
    full = lambda a: pl.BlockSpec(a.shape, lambda b, t: (0,) * a.ndim)
    tile = lambda dt: pltpu.VMEM((tt, D_RWKV), dt)
    per_chunk = lambda dt: pltpu.VMEM((tt // CHUNK, _N_GROUPS, GROUP_LANES, GROUP_LANES), dt)
    return pl.pallas_call(
        _rwkv_kernel,
        grid=(batch, nt),
        in_specs=[
            pl.BlockSpec((tt, RWKV_COLS), lambda b, t: (b * nt + t, 0)),
            full(wl), full(wg), full(ones_bd), full(tri),
            vec(D_RWKV), vec(D_RWKV), vec(D_RWKV), vec(D_RWKV), vec(D_RWKV), vec(D_RWKV), vec(D_RWKV),
        ],
        out_specs=pl.BlockSpec((tt, D_RWKV), lambda b, t: (b * nt + t, 0)),
        out_shape=jax.ShapeDtypeStruct((batch * seq, D_RWKV), _BF16),
        scratch_shapes=[
            pltpu.VMEM((_N_GROUPS, GROUP_LANES, GROUP_LANES), _F32),
            tile(_BF16), tile(_BF16), tile(_BF16), tile(_BF16), tile(_BF16), tile(_BF16), tile(_BF16),
            pltpu.VMEM((tt // CHUNK, D_RWKV), _F32),
            tile(_BF16), tile(_F32), per_chunk(_BF16),
            pltpu.VMEM((tt // CHUNK, _N_GROUPS, GROUP_LANES, 2 * CHUNK), _F32), per_chunk(_F32),
            tile(_F32), tile(_F32), tile(_F32),
        ],
        compiler_params=pltpu.CompilerParams(
            dimension_semantics=("arbitrary", "arbitrary"), vmem_limit_bytes=VMEM_LIMIT),
        name="rwkv",
    )(p, wl, wg, ones_bd, tri, w0, a0, k_k, k_a, r_k, ln_g, ln_b)


_N_FF_STEPS = D_FF // FFN_COLS
_PIECE_ROWS = 64
_PIECE_LANES = 128


_GELU_C1 = 0.7978845608028654
_GELU_C2 = _GELU_C1 * 0.044715


def _two_gelu_tanh(x):
    return x + x * jnp.tanh(x * (_GELU_C1 + _GELU_C2 * (x * x)))


def _out_ffn_kernel(attn_ref, rw_ref, x_ref, wo_ref, gmp_ref, gfp_ref, gfo_ref,
                    wup_ref, cw_ref, cb_ref, wdn_ref, out_ref,
                    carry_ref, h_ref, x1_ref, u_s, act_s):
    tm = x_ref.shape[0]

    @pl.when(pl.program_id(1) == 0)
    def _():
        carry_ref[...] = jnp.zeros_like(carry_ref)

    mix = _dot(attn_ref[...], wo_ref[0:D_ATTN, :]) + _dot(rw_ref[...], wo_ref[D_ATTN:D_MODEL, :])
    x1 = x_ref[...] + _rms_norm(mix, gmp_ref[...])
    x1_ref[...] = x1
    h_ref[...] = _rms_norm(x1, gfp_ref[...]).astype(_BF16)

    def up(c):
        slot = c % 2
        h = h_ref[...]
        for half, col0 in enumerate((c * FFN_COLS, D_FF + c * FFN_COLS)):
            u = jnp.dot(h, wup_ref[:, col0:col0 + FFN_COLS], preferred_element_type=_F32)
            lanes = slice(half * FFN_COLS, (half + 1) * FFN_COLS)
            u_s[slot, 8:tm + 8, lanes] = u
            u_s[slot, 0:8, lanes] = carry_ref[:, col0:col0 + FFN_COLS]
            carry_ref[:, col0:col0 + FFN_COLS] = u[tm - 8:tm]

    def conv_act(c):
        slot = c % 2

        def conv(r0, col, wcol):
            w = u_s[slot, r0:r0 + _PIECE_ROWS + 8, col:col + _PIECE_LANES]
            lanes = slice(wcol, wcol + _PIECE_LANES)
            r1 = pltpu.roll(w, 1, 0)[8:]
            r2 = pltpu.roll(w, 2, 0)[8:]
            return cb_ref[:, lanes] + cw_ref[0:1, lanes] * r2 + cw_ref[1:2, lanes] * r1 + cw_ref[2:3, lanes] * w[8:]

        for r0 in range(0, tm, _PIECE_ROWS):
            for col in range(0, FFN_COLS, _PIECE_LANES):
                gate = conv(r0, col, c * FFN_COLS + col)
                half_val = conv(r0, FFN_COLS + col, D_FF + c * FFN_COLS + col)
                act_s[slot, r0:r0 + _PIECE_ROWS, col:col + _PIECE_LANES] = (
                    _two_gelu_tanh(gate) * half_val).astype(_BF16)

    up(0)
    for c in range(_N_FF_STEPS):
        if c + 1 < _N_FF_STEPS:
            up(c + 1)
        conv_act(c)
        d = jnp.dot(act_s[c % 2], wdn_ref[c], preferred_element_type=_F32)
        if c == 0:
            out_ref[...] = d
        else:
            out_ref[...] += d
    out_ref[...] = x1_ref[...] + _rms_norm(out_ref[...], gfo_ref[...])


def _out_ffn(attn, rw, x2, wo, g_mix_post, g_ffn_pre, g_ffn_post, wup, cw, cb, wdn, batch, seq):
    tm = FFN_ROWS
    nt = seq // tm
    rows = lambda n: pl.BlockSpec((tm, n), lambda b, t: (b * nt + t, 0))
    vec = pl.BlockSpec((1, D_MODEL), lambda b, t: (0, 0))
    const = lambda a: pl.BlockSpec(a.shape, lambda b, t: (0,) * a.ndim, pipeline_mode=pl.Buffered(1))
    return pl.pallas_call(
        _out_ffn_kernel,
        grid=(batch, nt),
        in_specs=[rows(D_ATTN), rows(D_RWKV), rows(D_MODEL), const(wo), vec, vec, vec,
                  const(wup), const(cw), const(cb), const(wdn)],
        out_specs=rows(D_MODEL),
        out_shape=jax.ShapeDtypeStruct((batch * seq, D_MODEL), _F32),
        scratch_shapes=[
            pltpu.VMEM((8, 2 * D_FF), _F32),
            pltpu.VMEM((tm, D_MODEL), _BF16),
            pltpu.VMEM((tm, D_MODEL), _F32),
            pltpu.VMEM((2, tm + 8, 2 * FFN_COLS), _F32),
            pltpu.VMEM((2, tm, FFN_COLS), _BF16),
        ],
        compiler_params=pltpu.CompilerParams(
            dimension_semantics=("arbitrary", "arbitrary"), vmem_limit_bytes=VMEM_LIMIT),
        name="out_ffn",
    )(attn, rw, x2, wo, g_mix_post, g_ffn_pre, g_ffn_post, wup, cw, cb, wdn)


def _t5_bucket_table():
    rel = (np.arange(BLOCK)[:, None] + BLOCK) - np.arange(2 * BLOCK)[None, :]
    n = np.maximum(rel, 0)
    max_exact = N_BUCKETS // 2
    large = max_exact + (np.log(np.maximum(n, 1).astype(np.float32) / np.float32(max_exact))
                         / np.float32(math.log(MAX_DISTANCE / max_exact))
                         * np.float32(N_BUCKETS - max_exact)).astype(np.int32)
    large = np.minimum(large, N_BUCKETS - 1)
    return np.where(n < max_exact, n, large).astype(np.int32)


def _row(a):
    return a.reshape(1, -1)


def _in_proj_stage(x2, norm_mix_pre, w_in, shift_mix, seq):
    return _in_proj(x2, _row(norm_mix_pre), w_in.astype(_BF16), _row(shift_mix), seq)


def _rwkv_stage(p, w0, w_decay_up, a0, w_iclr_up, w_gate_up, k_k, k_a, r_k, ln_x_g, ln_x_b, batch, seq):
    zeros = jnp.zeros((LORA_DECAY, D_RWKV), _F32)
    w_lora = jnp.concatenate([jnp.concatenate([w_decay_up, zeros], axis=1),
                              jnp.concatenate([zeros, w_iclr_up], axis=1)], axis=0).astype(_BF16)
    hid = jnp.arange(D_RWKV) // HEAD_DIM
    ones_bd = (hid[:, None] == hid[None, :]).astype(_BF16)
    tri = (jnp.arange(CHUNK)[:, None] >= jnp.arange(CHUNK)[None, :]).astype(_BF16)
    return _rwkv(p, w_lora, w_gate_up.astype(_BF16), ones_bd, tri, _row(w0), _row(a0), _row(k_k),
                 _row(k_a), _row(r_k), _row(ln_x_g), _row(ln_x_b), batch, seq)


def _ffn_stage(attn, rw, x2, norm_mix_post, norm_ffn_pre, norm_ffn_post, w_out, w_ffn_up, conv_w, conv_b,
               w_ffn_down, batch, seq):
    half_val = jnp.concatenate([jnp.ones((D_FF,), _F32), jnp.full((D_FF,), 0.5, _F32)])
    cw = jnp.pad(conv_w * half_val, ((0, 8 - conv_w.shape[0]), (0, 0)))
    cb = _row(conv_b * half_val)
    wdn = w_ffn_down.reshape(_N_FF_STEPS, FFN_COLS, D_MODEL).astype(_BF16)
    return _out_ffn(attn, rw, x2, w_out.astype(_BF16), _row(norm_mix_post), _row(norm_ffn_pre),
                    _row(norm_ffn_post), w_ffn_up.astype(_BF16), cw, cb, wdn, batch, seq)


def _layer(x2, batch, seq, norm_mix_pre, norm_mix_post, norm_ffn_pre, norm_ffn_post, w_in, bias, sinks,
           shift_mix, w0, w_decay_up, a0, w_iclr_up, w_gate_up, k_k, k_a, r_k, ln_x_g, ln_x_b,
           w_out, w_ffn_up, conv_w, conv_b, w_ffn_down):
    q, kd, vd, p = _in_proj_stage(x2, norm_mix_pre, w_in, shift_mix, seq)
    attn = _attention(sinks, q, kd, vd, bias, batch, seq)
    rw = _rwkv_stage(p, w0, w_decay_up, a0, w_iclr_up, w_gate_up, k_k, k_a, r_k, ln_x_g, ln_x_b,
                     batch, seq)
    return _ffn_stage(attn, rw, x2, norm_mix_post, norm_ffn_pre, norm_ffn_post, w_out, w_ffn_up, conv_w, conv_b,
                      w_ffn_down, batch, seq)


def kernel(x, norm_mix_pre, norm_mix_post, norm_ffn_pre, norm_ffn_post, w_in, rel_bias, sinks, rwkv_shift_mix, w0, w_decay_up, a0, w_iclr_up, w_gate_up, k_k, k_a, r_k, ln_x_g, ln_x_b, w_out, w_ffn_up, conv_w, conv_b, w_ffn_down):
    batch, seq, _ = x.shape
    depth = w_in.shape[0]
    bias = _bias_table(rel_bias.astype(_F32), jnp.asarray(_t5_bucket_table()))
    bias = bias.reshape(N_Q_HEADS // 2, 2 * BLOCK, 2 * BLOCK)
    x2 = x.reshape(batch * seq, D_MODEL)
    for l in range(depth):
        x2 = _layer(x2, batch, seq, norm_mix_pre[l], norm_mix_post[l], norm_ffn_pre[l], norm_ffn_post[l], w_in[l],
                    bias, sinks[l], rwkv_shift_mix[l], w0[l], w_decay_up[l], a0[l], w_iclr_up[l], w_gate_up[l],
                    k_k[l], k_a[l], r_k[l].reshape(-1), ln_x_g[l], ln_x_b[l], w_out[l], w_ffn_up[l], conv_w[l],
                    conv_b[l], w_ffn_down[l])
    return x2.reshape(batch, seq, D_MODEL)
```
